```python
import math
import jax
import jax.numpy as jnp
from jax import lax
import numpy as np

D_MODEL = 1024
BATCH = 1
SEQ = 16384
DEPTH = 2
DEC_BATCH = 32
DEC_SEQ = 1
PAST_LEN = 16384
PAGE_SIZE = 128

N_A_LAYERS = DEPTH // 2
N_B_LAYERS = DEPTH - N_A_LAYERS
SSM_GROUP = 16
SSM_GROUPS = D_MODEL // SSM_GROUP
SSM_STATE = 64
DT_MIN = 1e-3
DT_MAX = 1e-1
N_HEADS = 16
HEAD_DIM = D_MODEL // N_HEADS
Q_BLOCK = 128
N_EXPERT_GROUPS = 4
EXPERTS_PER_GROUP = 8
N_EXPERTS = N_EXPERT_GROUPS * EXPERTS_PER_GROUP
TOP_K = 2
D_EXPERT = D_MODEL // 4
EPS = 1e-6

kernel_name = "yoco_s5_fox_hmoe_step"


def rmsnorm(x, g):
    xf = x.astype(jnp.float32)
    y = xf * lax.rsqrt(jnp.mean(xf * xf, axis=-1, keepdims=True) + EPS)
    return (y * g.astype(jnp.float32)).astype(x.dtype)


def s5_mixer(u, s0_re, s0_im, lam_re, lam_im, log_dt, b_re, b_im, c_re, c_im, d_skip, w_glu, b_glu):
    bsz, t, _ = u.shape
    f32 = jnp.float32
    uf = u.astype(f32).reshape(bsz, t, SSM_GROUPS, SSM_GROUP)
    lam = lax.complex(lam_re.astype(f32), lam_im.astype(f32))
    dt = jnp.exp(log_dt.astype(f32))[:, None]
    a_bar = jnp.exp(lam * dt)
    b_bar = ((a_bar - 1.0) / lam)[..., None] * lax.complex(b_re.astype(f32), b_im.astype(f32))
    bu = jnp.einsum('gph,btgh->btgp', b_bar, uf.astype(jnp.complex64))
    s0 = lax.complex(s0_re.astype(f32), s0_im.astype(f32))
    bu = bu.at[:, 0].add(a_bar * s0)
    a_full = jnp.broadcast_to(a_bar, bu.shape)

    def combine(e1, e2):
        a1, b1 = e1
        a2, b2 = e2
        return a2 * a1, a2 * b1 + b2

    _, s = lax.associative_scan(combine, (a_full, bu), axis=1)
    cmat = lax.complex(c_re.astype(f32), c_im.astype(f32))
    y = jnp.einsum('ghp,btgp->btgh', cmat, s).real + d_skip.astype(f32).reshape(SSM_GROUPS, SSM_GROUP) * uf
    z = jax.nn.gelu(y.reshape(bsz, t, D_MODEL))
    zz = z @ w_glu.astype(f32) + b_glu.astype(f32)
    out = zz[..., :D_MODEL] * jax.nn.sigmoid(zz[..., D_MODEL:])
    s_last = s[:, -1]
    return out.astype(u.dtype), s_last.real, s_last.imag


def hier_moe(x, w_rg, w_re, w_gate, w_up, w_down):
    bsz, t, d = x.shape
    f32 = jnp.float32
    xt = x.reshape(-1, d)
    g_logits = (xt @ w_rg).astype(f32)
    g_prob = jax.nn.softmax(g_logits, axis=-1)
    g_idx = jnp.argmax(g_logits, axis=-1)
    g_w = jnp.take_along_axis(g_prob, g_idx[:, None], axis=-1)
    e_logits = (xt @ w_re).astype(f32).reshape(-1, N_EXPERT_GROUPS, EXPERTS_PER_GROUP)
    e_sel = jnp.take_along_axis(e_logits, g_idx[:, None, None], axis=1)[:, 0]
    top_v, top_i = lax.top_k(e_sel, TOP_K)
    top_p = jax.nn.softmax(top_v, axis=-1)
    w_in = jnp.sum(top_p[..., None] * jax.nn.one_hot(top_i, EXPERTS_PER_GROUP, dtype=f32), axis=1)
    comb = (jax.nn.one_hot(g_idx, N_EXPERT_GROUPS, dtype=f32)[:, :, None] * (g_w * w_in)[:, None, :]).reshape(-1, N_EXPERTS)
    h = jax.nn.silu(jnp.einsum('nd,edf->nef', xt, w_gate)) * jnp.einsum('nd,edf->nef', xt, w_up)
    h = h * comb[..., None].astype(h.dtype)
    out = jnp.einsum('nef,efd->nd', h, w_down)
    return out.reshape(bsz, t, d)


def shared_kv(h, kv_norm, w_kvf, b_forget, k_norm):
    bsz, t, _ = h.shape
    hd = N_HEADS * HEAD_DIM
    proj = rmsnorm(h, kv_norm) @ w_kvf
    k = rmsnorm(proj[..., :hd].reshape(bsz, t, N_HEADS, HEAD_DIM), k_norm)
    v = proj[..., hd:2 * hd].reshape(bsz, t, N_HEADS, HEAD_DIM)
    logf = jax.nn.log_sigmoid(proj[..., 2 * hd:].astype(jnp.float32) + b_forget.astype(jnp.float32))
    return k, v, logf


def fox_prompt(q, k, v, logf):
    bsz, seq_len = q.shape[0], q.shape[1]
    scale = HEAD_DIM ** -0.5
    cum = jnp.cumsum(logf, axis=1).transpose(0, 2, 1)
    kpos = jnp.arange(seq_len)

    def one_block(i):
        start = i * Q_BLOCK
        qb = lax.dynamic_slice_in_dim(q, start, Q_BLOCK, axis=1)
        cq = lax.dynamic_slice_in_dim(cum, start, Q_BLOCK, axis=2)
        s = jnp.einsum('bqhd,bkhd->bhqk', qb, k).astype(jnp.float32) * scale + (cq[..., None] - cum[:, :, None, :])
        qpos = start + jnp.arange(Q_BLOCK)
        s = jnp.where(kpos[None, :] <= qpos[:, None], s, -jnp.inf)
        p = jax.nn.softmax(s, axis=-1).astype(v.dtype)
        return jnp.einsum('bhqk,bkhd->bqhd', p, v)

    out = lax.map(one_block, jnp.arange(seq_len // Q_BLOCK))
    return out.transpose(1, 0, 2, 3, 4).reshape(bsz, seq_len, N_HEADS * HEAD_DIM)


def fox_sample(q, k_new, v_new, logf_new, k_past, v_past, cum_past):
    t = q.shape[1]
    n_past = k_past.shape[1]
    scale = HEAD_DIM ** -0.5
    cq = (cum_past[:, -1:, :] + jnp.cumsum(logf_new, axis=1)).transpose(0, 2, 1)
    ck_past = cum_past.transpose(0, 2, 1)
    s_past = jnp.einsum('bthd,bshd->bhts', q, k_past).astype(jnp.float32) * scale + (cq[..., None] - ck_past[:, :, None, :])
    s_new = jnp.einsum('bthd,bshd->bhts', q, k_new).astype(jnp.float32) * scale + (cq[..., None] - cq[:, :, None, :])
    causal = jnp.tril(jnp.ones((t, t), dtype=bool))
    s_new = jnp.where(causal, s_new, -jnp.inf)
    p = jax.nn.softmax(jnp.concatenate([s_past, s_new], axis=-1), axis=-1).astype(v_new.dtype)
    out = jnp.einsum('bhts,bshd->bthd', p[..., :n_past], v_past) + jnp.einsum('bhts,bshd->bthd', p[..., n_past:], v_new)
    return out.reshape(q.shape[0], t, N_HEADS * HEAD_DIM)


def setup_inputs(seed: int = 0) -> dict:
    key = jax.random.key(seed)
    ks = iter(jax.random.split(key, 40))

    def nrm(shape, scale):
        return scale * jax.random.normal(next(ks), shape, jnp.float32)

    n_pages = PAST_LEN // PAGE_SIZE
    n_used = DEC_BATCH * n_pages
    n_pool = n_used + max(1, n_used // 4)
    hd = N_HEADS * HEAD_DIM
    inp = {}
    inp["x_prompt"] = nrm((BATCH, SEQ, D_MODEL), 1.0)
    inp["x_sample"] = nrm((DEC_BATCH, DEC_SEQ, D_MODEL), 1.0)
    inp["state_ssm_re"] = nrm((N_A_LAYERS, DEC_BATCH, SSM_GROUPS, SSM_STATE), 0.1)
    inp["state_ssm_im"] = nrm((N_A_LAYERS, DEC_BATCH, SSM_GROUPS, SSM_STATE), 0.1)
    inp["cache_k"] = nrm((n_pool, PAGE_SIZE, N_HEADS, HEAD_DIM), 1.0)
    inp["cache_v"] = nrm((n_pool, PAGE_SIZE, N_HEADS, HEAD_DIM), 1.0)
    inp["cache_logf"] = jax.nn.log_sigmoid(nrm((n_pool, PAGE_SIZE, N_HEADS), 1.0))
    inp["page_table"] = jax.random.permutation(next(ks), n_pool)[:n_used].reshape(DEC_BATCH, n_pages).astype(jnp.int32)
    inp["norm_mix"] = 1.0 + nrm((DEPTH, D_MODEL), 0.02)
    inp["norm_ffn"] = 1.0 + nrm((DEPTH, D_MODEL), 0.02)
    inp["ssm_lambda_re"] = -0.5 + nrm((N_A_LAYERS, SSM_GROUPS, SSM_STATE), 0.01)
    inp["ssm_lambda_im"] = jnp.pi * jnp.arange(SSM_STATE, dtype=jnp.float32) + nrm((N_A_LAYERS, SSM_GROUPS, SSM_STATE), 0.01)
    inp["ssm_log_dt"] = jax.random.uniform(next(ks), (N_A_LAYERS, SSM_GROUPS), jnp.float32, math.log(DT_MIN), math.log(DT_MAX))
    inp["ssm_b_re"] = nrm((N_A_LAYERS, SSM_GROUPS, SSM_STATE, SSM_GROUP), (2 * SSM_GROUP) ** -0.5)
    inp["ssm_b_im"] = nrm((N_A_LAYERS, SSM_GROUPS, SSM_STATE, SSM_GROUP), (2 * SSM_GROUP) ** -0.5)
    inp["ssm_c_re"] = nrm((N_A_LAYERS, SSM_GROUPS, SSM_GROUP, SSM_STATE), (2 * SSM_STATE) ** -0.5)
    inp["ssm_c_im"] = nrm((N_A_LAYERS, SSM_GROUPS, SSM_GROUP, SSM_STATE), (2 * SSM_STATE) ** -0.5)
    inp["ssm_d"] = nrm((N_A_LAYERS, D_MODEL), 1.0)
    inp["ssm_w_glu"] = nrm((N_A_LAYERS, D_MODEL, 2 * D_MODEL), D_MODEL ** -0.5)
    inp["ssm_b_glu"] = nrm((N_A_LAYERS, 2 * D_MODEL), 0.01)
    inp["kv_norm"] = 1.0 + nrm((D_MODEL,), 0.02)
    inp["w_kvf"] = nrm((D_MODEL, 2 * hd + N_HEADS), D_MODEL ** -0.5)
    inp["b_forget"] = nrm((N_HEADS,), 0.01)
    inp["k_norm"] = 1.0 + nrm((HEAD_DIM,), 0.02)
    inp["w_q"] = nrm((N_B_LAYERS, D_MODEL, hd), D_MODEL ** -0.5)
    inp["q_norm"] = 1.0 + nrm((N_B_LAYERS, HEAD_DIM), 0.02)
    inp["w_o"] = nrm((N_B_LAYERS, hd, D_MODEL), hd ** -0.5)
    inp["w_router_group"] = nrm((DEPTH, D_MODEL, N_EXPERT_GROUPS), D_MODEL ** -0.5)
    inp["w_router_expert"] = nrm((DEPTH, D_MODEL, N_EXPERTS), D_MODEL ** -0.5)
    inp["w_gate"] = nrm((DEPTH, N_EXPERTS, D_MODEL, D_EXPERT), D_MODEL ** -0.5)
    inp["w_up"] = nrm((DEPTH, N_EXPERTS, D_MODEL, D_EXPERT), D_MODEL ** -0.5)
    inp["w_down"] = nrm((DEPTH, N_EXPERTS, D_EXPERT, D_MODEL), D_EXPERT ** -0.5)
    return inp


def reference(x_prompt, x_sample, state_ssm_re, state_ssm_im, cache_k, cache_v, cache_logf, page_table,
              norm_mix, norm_ffn, ssm_lambda_re, ssm_lambda_im, ssm_log_dt, ssm_b_re, ssm_b_im,
              ssm_c_re, ssm_c_im, ssm_d, ssm_w_glu, ssm_b_glu, kv_norm, w_kvf, b_forget, k_norm,
              w_q, q_norm, w_o, w_router_group, w_router_expert, w_gate, w_up, w_down):

    def trunk(x, s_re0, s_im0, attend):
        bsz, t, _ = x.shape
        new_re, new_im = [], []
        kv = None
        for layer in range(DEPTH):
            if layer < N_A_LAYERS:
                a = layer
                y, sr, si = s5_mixer(rmsnorm(x, norm_mix[layer]), s_re0[a], s_im0[a],
                                     ssm_lambda_re[a], ssm_lambda_im[a], ssm_log_dt[a],
                                     ssm_b_re[a], ssm_b_im[a], ssm_c_re[a], ssm_c_im[a],
                                     ssm_d[a], ssm_w_glu[a], ssm_b_glu[a])
                new_re.append(sr)
                new_im.append(si)
                x = x + y
            else:
                j = layer - N_A_LAYERS
                q = (rmsnorm(x, norm_mix[layer]) @ w_q[j]).reshape(bsz, t, N_HEADS, HEAD_DIM)
                q = rmsnorm(q, q_norm[j])
                x = x + attend(q, kv) @ w_o[j]
            x = x + hier_moe(rmsnorm(x, norm_ffn[layer]), w_router_group[layer], w_router_expert[layer],
                             w_gate[layer], w_up[layer], w_down[layer])
            if layer == N_A_LAYERS - 1:
                kv = shared_kv(x, kv_norm, w_kvf, b_forget, k_norm)
        return x, jnp.stack(new_re), jnp.stack(new_im), kv

    zeros_state = jnp.zeros((N_A_LAYERS, x_prompt.shape[0], SSM_GROUPS, SSM_STATE), jnp.float32)
    y_prompt, ssm_re_prompt, ssm_im_prompt, kv_p = trunk(
        x_prompt, zeros_state, zeros_state, lambda q, kv: fox_prompt(q, kv[0], kv[1], kv[2]))
    k_prompt, v_prompt, logf_prompt = kv_p

    dec_b, n_pages = page_table.shape
    n_past = n_pages * cache_k.shape[1]
    k_past = cache_k[page_table].reshape(dec_b, n_past, N_HEADS, HEAD_DIM)
    v_past = cache_v[page_table].reshape(dec_b, n_past, N_HEADS, HEAD_DIM)
    cum_past = jnp.cumsum(cache_logf[page_table].reshape(dec_b, n_past, N_HEADS).astype(jnp.float32), axis=1)
    y_sample, ssm_re_sample, ssm_im_sample, kv_s = trunk(
        x_sample, state_ssm_re, state_ssm_im,
        lambda q, kv: fox_sample(q, kv[0], kv[1], kv[2], k_past, v_past, cum_past))
    k_sample, v_sample, logf_sample = kv_s

    return (y_prompt, y_sample, ssm_re_prompt, ssm_im_prompt, k_prompt, v_prompt, logf_prompt,
            ssm_re_sample, ssm_im_sample, k_sample, v_sample, logf_sample)
```

```python
import functools

import numpy as np
import jax
import jax.numpy as jnp
from jax import lax
from jax.experimental import pallas as pl
from jax.experimental.pallas import tpu as pltpu

F32 = jnp.float32
BF16 = jnp.bfloat16
HI = lax.Precision.HIGHEST

D_MODEL = 1024
SSM_GROUP = 16
SSM_GROUPS = 64
SSM_STATE = 64
N_HEADS = 16
HEAD_DIM = 64
N_EXPERT_GROUPS = 4
EXPERTS_PER_GROUP = 8
N_EXPERTS = 32
D_EXPERT = 256
EPS = 1e-6
NEG = -1e30

V7X_VMEM_BYTES = 64 * 1024 * 1024
LANES = 128
SUBLANES = 8

SEG_COUNT = SUBLANES
SEG_LEN = 32
S5_TILE = SEG_COUNT * SEG_LEN
S5_BLOCKS = 4
S5_BLOCK_CH = D_MODEL // S5_BLOCKS
S5_BLOCK_ST = 2 * SSM_GROUPS * SSM_STATE // S5_BLOCKS
S5_CHUNK = 512
S5_NCHUNK = 2 * SSM_GROUPS * SSM_STATE // S5_CHUNK
S5_NPAIR = S5_NCHUNK // 2
S5_EXPS = (1, 0) + tuple(SEG_LEN * q for q in range(1, SEG_COUNT))

ATT_BLOCK = 256
SKIP_LOGIT_GAP = 100.0


def _cparams(semantics, vmem_mb):
    return pltpu.CompilerParams(dimension_semantics=semantics,
                                vmem_limit_bytes=min(vmem_mb * 1024 * 1024, V7X_VMEM_BYTES - 4 * 1024 * 1024))


def _dot(a, b, hi):
    if hi:
        return jnp.dot(a, b, precision=HI, preferred_element_type=F32)
    return jnp.dot(a.astype(BF16), b.astype(BF16), preferred_element_type=F32)


def _sigmoid(x):
    return 1.0 / (1.0 + jnp.exp(-x))


def _log_sigmoid(x):
    return jnp.minimum(x, 0.0) - jnp.log(1.0 + jnp.exp(-jnp.abs(x)))


def _gelu_tanh(x):
    c = np.float32(np.sqrt(2.0 / np.pi))
    return 0.5 * x * (1.0 + jnp.tanh(c * (x + 0.044715 * (x * x * x))))


def _rmsnorm_rows(x, gain):
    ms = jnp.mean(x * x, axis=-1, keepdims=True)
    return x * lax.rsqrt(ms + EPS) * gain


def _ssm_prep_kernel(lam_re_ref, lam_im_ref, log_dt_ref, lam_re_x_ref, lam_im_x_ref, b_re_ref, b_im_ref,
                     apow_re_ref, apow_im_ref, bb_re_ref, bb_im_ref):
    dt = jnp.exp(log_dt_ref[...])

    def disc(lr, li):
        mag = jnp.exp(lr * dt)
        ang = li * dt
        return mag * jnp.cos(ang), mag * jnp.sin(ang)

    def cmul(a, b):
        return a[0] * b[0] - a[1] * b[1], a[0] * b[1] + a[1] * b[0]

    a1 = disc(lam_re_ref[...], lam_im_ref[...])
    a_seg = a1
    for _ in range(int(np.log2(SEG_LEN))):
        a_seg = cmul(a_seg, a_seg)
    powers = {1: a1, 0: (jnp.ones_like(a1[0]), jnp.zeros_like(a1[0])), SEG_LEN: a_seg}
    for q in range(2, SEG_COUNT):
        powers[SEG_LEN * q] = cmul(powers[SEG_LEN * (q - 1)], a_seg)
    for i, e in enumerate(S5_EXPS):
        apow_re_ref[i] = powers[e][0]
        apow_im_ref[i] = powers[e][1]

    lrx, lix = lam_re_x_ref[...], lam_im_x_ref[...]
    arx, aix = disc(lrx, lix)
    den = lrx * lrx + lix * lix
    nr, ni = arx - 1.0, aix
    wr = (nr * lrx + ni * lix) / den
    wi = (ni * lrx - nr * lix) / den
    br, bi = b_re_ref[...], b_im_ref[...]
    bb_re_ref[...] = wr * br - wi * bi
    bb_im_ref[...] = wr * bi + wi * br


def _ssm_prep(lam_re, lam_im, log_dt, b_re, b_im):
    g, p, h = SSM_GROUPS, SSM_STATE, SSM_GROUP
    rep = lambda a: jnp.repeat(a, h, axis=1)
    n_exp = len(S5_EXPS)
    out_shape = (jax.ShapeDtypeStruct((n_exp, g, p), F32), jax.ShapeDtypeStruct((n_exp, g, p), F32),
                 jax.ShapeDtypeStruct((g, p * h), F32), jax.ShapeDtypeStruct((g, p * h), F32))
    apow_re, apow_im, bb_re, bb_im = pl.pallas_call(
        _ssm_prep_kernel, out_shape=out_shape, name="ssm_prep",
    )(lam_re, lam_im, log_dt.reshape(g, 1), rep(lam_re), rep(lam_im),
      b_re.reshape(g, p * h), b_im.reshape(g, p * h))
    return apow_re, apow_im, bb_re.reshape(g, p, h), bb_im.reshape(g, p, h)


def _ssm_layouts(apow_re, apow_im, bb_re, bb_im, c_re, c_im):
    nb, gl = S5_BLOCKS, SSM_GROUPS // S5_BLOCKS
    eye = jnp.eye(gl, dtype=F32)

    def embed_b(bb):
        v = bb.transpose(0, 2, 1).reshape(nb, gl, SSM_GROUP, SSM_STATE)
        return jnp.einsum('jghp,gk->jghkp', v, eye).reshape(nb, gl * SSM_GROUP, gl * SSM_STATE)

    def embed_c(c):
        v = c.transpose(0, 2, 1).reshape(nb, gl, SSM_STATE, SSM_GROUP)
        return jnp.einsum('jgph,gk->jgpkh', v, eye).reshape(nb, gl * SSM_STATE, gl * SSM_GROUP)

    bmat = jnp.concatenate([embed_b(bb_re), embed_b(bb_im)], axis=2)
    cmat = jnp.concatenate([embed_c(c_re), -embed_c(c_im)], axis=1)

    flat = lambda a: a.reshape(a.shape[0], S5_NPAIR, S5_CHUNK)
    pr, pi = flat(apow_re), flat(apow_im)
    idx = {e: i for i, e in enumerate(S5_EXPS)}
    rows = [pr[idx[1]], pi[idx[1]]]
    for d in (1, 2, 4):
        rows += [pr[idx[SEG_LEN * d]], pi[idx[SEG_LEN * d]]]
    rows += [pr[idx[SEG_LEN * q]] for q in range(SEG_COUNT)]
    rows += [pi[idx[SEG_LEN * q]] for q in range(SEG_COUNT)]
    consts = jnp.stack(rows, axis=1)
    a_rows = jnp.stack([apow_re[idx[1]].reshape(-1), apow_im[idx[1]].reshape(-1)])
    return bmat, cmat, consts, a_rows


def _state_to_layout(s_re, s_im):
    b = s_re.shape[0]
    r = s_re.reshape(b, S5_BLOCKS, 1, S5_BLOCK_ST // 2)
    i = s_im.reshape(b, S5_BLOCKS, 1, S5_BLOCK_ST // 2)
    return jnp.concatenate([r, i], axis=2).reshape(b, S5_BLOCKS * S5_BLOCK_ST)


def _state_from_layout(s):
    b = s.shape[0]
    v = s.reshape(b, S5_BLOCKS, 2, S5_BLOCK_ST // 2)
    return (v[:, :, 0].reshape(b, SSM_GROUPS, SSM_STATE), v[:, :, 1].reshape(b, SSM_GROUPS, SSM_STATE))


def _segment_permutation():
    m = np.zeros((S5_TILE, S5_TILE), np.float32)
    for rp in range(S5_TILE):
        k, q = divmod(rp, SEG_COUNT)
        m[rp, q * SEG_LEN + k] = 1.0
    return m


def _s5_prompt_kernel(x_ref, gain_ref, perm_ref, perm_t_ref, bmat_ref, cmat_ref, const_ref, dskip_ref,
                      wglu_ref, bglu_ref, o_ref, state_ref, pbuf, sbuf, carry):
    i = pl.program_id(0)

    @pl.when(i == 0)
    def _():
        carry[...] = jnp.zeros_like(carry)

    x = x_ref[...]
    u = _rmsnorm_rows(x, gain_ref[...])
    up = jnp.dot(perm_ref[...], u.astype(BF16), preferred_element_type=F32).astype(BF16)
    for j in range(S5_BLOCKS):
        pj = jnp.dot(up[:, j * S5_BLOCK_CH:(j + 1) * S5_BLOCK_CH], bmat_ref[j], preferred_element_type=F32)
        for c in range(S5_BLOCK_ST // S5_CHUNK):
            pbuf[j * (S5_BLOCK_ST // S5_CHUNK) + c] = pj[:, c * S5_CHUNK:(c + 1) * S5_CHUNK]

    row = lax.broadcasted_iota(jnp.int32, (SEG_COUNT, S5_CHUNK), 0)

    def chunk_pair(cp, _):
        c_re = (cp // 2) * 4 + (cp % 2)
        c_im = c_re + 2
        cst = const_ref[cp]
        ar = jnp.broadcast_to(cst[0:1], (SEG_COUNT, S5_CHUNK))
        ai = jnp.broadcast_to(cst[1:2], (SEG_COUNT, S5_CHUNK))

        def step(k, sr, si):
            pr = pbuf[c_re, k * SEG_COUNT:(k + 1) * SEG_COUNT, :]
            pi = pbuf[c_im, k * SEG_COUNT:(k + 1) * SEG_COUNT, :]
            return ar * sr - ai * si + pr, ar * si + ai * sr + pi

        er = jnp.zeros((SEG_COUNT, S5_CHUNK), F32)
        ei = jnp.zeros((SEG_COUNT, S5_CHUNK), F32)
        for k in range(SEG_LEN):
            er, ei = step(k, er, ei)

        for n, d in enumerate((1, 2, 4)):
            cr, ci = cst[2 + 2 * n:3 + 2 * n], cst[3 + 2 * n:4 + 2 * n]
            shr = jnp.where(row >= d, pltpu.roll(er, d, 0), 0.0)
            shi = jnp.where(row >= d, pltpu.roll(ei, d, 0), 0.0)
            er, ei = er + cr * shr - ci * shi, ei + cr * shi + ci * shr
        exr = jnp.where(row >= 1, pltpu.roll(er, 1, 0), 0.0)
        exi = jnp.where(row >= 1, pltpu.roll(ei, 1, 0), 0.0)
        c0r = carry[c_re, SEG_COUNT - 1:SEG_COUNT, :]
        c0i = carry[c_im, SEG_COUNT - 1:SEG_COUNT, :]
        qr, qi = cst[8:16], cst[16:24]
        sr = exr + qr * c0r - qi * c0i
        si = exi + qr * c0i + qi * c0r

        for k2 in range(SEG_LEN // 2):
            sr1, si1 = step(2 * k2, sr, si)
            sr, si = step(2 * k2 + 1, sr1, si1)
            rows = slice(2 * k2 * SEG_COUNT, (2 * k2 + 2) * SEG_COUNT)
            sbuf[c_re, rows, :] = jnp.concatenate([sr1, sr], axis=0).astype(BF16)
            sbuf[c_im, rows, :] = jnp.concatenate([si1, si], axis=0).astype(BF16)
        carry[c_re] = sr
        carry[c_im] = si
        return 0

    lax.fori_loop(0, S5_NPAIR, chunk_pair, 0)

    ys = []
    for j in range(S5_BLOCKS):
        acc = None
        for c in range(S5_BLOCK_ST // S5_CHUNK):
            t = jnp.dot(sbuf[j * (S5_BLOCK_ST // S5_CHUNK) + c], cmat_ref[j, c * S5_CHUNK:(c + 1) * S5_CHUNK, :],
                        preferred_element_type=F32)
            acc = t if acc is None else acc + t
        ys.append(acc)
    yp = jnp.concatenate(ys, axis=1)
    yh = yp.astype(BF16)
    yl = (yp - yh.astype(F32)).astype(BF16)
    pt = perm_t_ref[...]
    y = (jnp.dot(pt, yh, preferred_element_type=F32) + jnp.dot(pt, yl, preferred_element_type=F32)
         + dskip_ref[...] * u)
    z = _gelu_tanh(y)
    zz = jnp.dot(z.astype(BF16), wglu_ref[...], preferred_element_type=F32) + bglu_ref[...]
    o_ref[...] = x + zz[:, :D_MODEL] * _sigmoid(zz[:, D_MODEL:])

    @pl.when(i == pl.num_programs(0) - 1)
    def _():
        state_ref[...] = carry[...]


def _s5_prompt(x, gain, bmat, cmat, consts, dskip, wglu, bglu):
    t = x.shape[0]
    assert t % S5_TILE == 0
    perm = _segment_permutation()
    const_spec = lambda shape: pl.BlockSpec(shape, lambda i: (0,) * len(shape))
    y, state = pl.pallas_call(
        _s5_prompt_kernel,
        grid=(t // S5_TILE,),
        in_specs=[pl.BlockSpec((S5_TILE, D_MODEL), lambda i: (i, 0)),
                  const_spec((1, D_MODEL)),
                  const_spec((S5_TILE, S5_TILE)), const_spec((S5_TILE, S5_TILE)),
                  const_spec(bmat.shape), const_spec(cmat.shape), const_spec(consts.shape),
                  const_spec((1, D_MODEL)),
                  const_spec((D_MODEL, 2 * D_MODEL)), const_spec((1, 2 * D_MODEL))],
        out_specs=[pl.BlockSpec((S5_TILE, D_MODEL), lambda i: (i, 0)),
                   const_spec((S5_NCHUNK, SEG_COUNT, S5_CHUNK))],
        out_shape=[jax.ShapeDtypeStruct((t, D_MODEL), F32),
                   jax.ShapeDtypeStruct((S5_NCHUNK, SEG_COUNT, S5_CHUNK), F32)],
        scratch_shapes=[pltpu.VMEM((S5_NCHUNK, S5_TILE, S5_CHUNK), F32),
                        pltpu.VMEM((S5_NCHUNK, S5_TILE, S5_CHUNK), BF16),
                        pltpu.VMEM((S5_NCHUNK, SEG_COUNT, S5_CHUNK), F32)],
        compiler_params=_cparams(("arbitrary",), 56),
        name="s5_prompt",
    )(x, gain.reshape(1, -1), jnp.asarray(perm, BF16), jnp.asarray(perm.T, BF16),
      bmat.astype(BF16), cmat.astype(BF16), consts, dskip.reshape(1, -1), wglu.astype(BF16), bglu.reshape(1, -1))
    s = state[:, SEG_COUNT - 1, :].reshape(1, S5_BLOCKS * S5_BLOCK_ST)
    return y, s


def _s5_sample_kernel(x_ref, gain_ref, bmat_ref, cmat_ref, a_ref, s0_ref, dskip_ref, wglu_ref, bglu_ref,
                      o_ref, s_ref):
    x = x_ref[...]
    u = _rmsnorm_rows(x, gain_ref[...])
    half = S5_BLOCK_ST // 2
    ys = []
    for j in range(S5_BLOCKS):
        pj = jnp.dot(u[:, j * S5_BLOCK_CH:(j + 1) * S5_BLOCK_CH], bmat_ref[j], precision=HI,
                     preferred_element_type=F32)
        ar = a_ref[0:1, j * half:(j + 1) * half]
        ai = a_ref[1:2, j * half:(j + 1) * half]
        s0r = s0_ref[:, j * S5_BLOCK_ST:j * S5_BLOCK_ST + half]
        s0i = s0_ref[:, j * S5_BLOCK_ST + half:(j + 1) * S5_BLOCK_ST]
        sr = ar * s0r - ai * s0i + pj[:, :half]
        si = ar * s0i + ai * s0r + pj[:, half:]
        s_ref[:, j * S5_BLOCK_ST:j * S5_BLOCK_ST + half] = sr
        s_ref[:, j * S5_BLOCK_ST + half:(j + 1) * S5_BLOCK_ST] = si
        ys.append(jnp.dot(sr, cmat_ref[j, :half, :], precision=HI, preferred_element_type=F32)
                  + jnp.dot(si, cmat_ref[j, half:, :], precision=HI, preferred_element_type=F32))
    y = jnp.concatenate(ys, axis=1) + dskip_ref[...] * u
    z = _gelu_tanh(y)
    zz = jnp.dot(z, wglu_ref[...], precision=HI, preferred_element_type=F32) + bglu_ref[...]
    o_ref[...] = x + zz[:, :D_MODEL] * _sigmoid(zz[:, D_MODEL:])


def _s5_sample(x, gain, bmat, cmat, a_rows, s0, dskip, wglu, bglu):
    n = x.shape[0]
    return pl.pallas_call(
        _s5_sample_kernel,
        out_shape=[jax.ShapeDtypeStruct((n, D_MODEL), F32), jax.ShapeDtypeStruct(s0.shape, F32)],
        compiler_params=_cparams(None, 56),
        name="s5_sample",
    )(x, gain.reshape(1, -1), bmat, cmat, a_rows, s0, dskip.reshape(1, -1), wglu, bglu.reshape(1, -1))


def _route(xn, wrg, wre):
    gl = jnp.dot(xn, wrg, precision=HI, preferred_element_type=F32)
    el = jnp.dot(xn, wre, precision=HI, preferred_element_type=F32)
    rows = xn.shape[0]
    lane_g = lax.broadcasted_iota(jnp.int32, (rows, N_EXPERT_GROUPS), 1)
    gmax = jnp.max(gl, axis=1, keepdims=True)
    gidx = jnp.min(jnp.where(gl == gmax, lane_g, N_EXPERT_GROUPS), axis=1, keepdims=True)
    gw = 1.0 / jnp.sum(jnp.exp(gl - gmax), axis=1, keepdims=True)
    lane_e = lax.broadcasted_iota(jnp.int32, (rows, N_EXPERTS), 1)
    in_group = lax.shift_right_logical(lane_e, 3) == gidx
    m1 = jnp.where(in_group, el, -jnp.inf)
    v1 = jnp.max(m1, axis=1, keepdims=True)
    i1 = jnp.min(jnp.where(m1 == v1, lane_e, N_EXPERTS), axis=1, keepdims=True)
    m2 = jnp.where(lane_e == i1, -jnp.inf, m1)
    v2 = jnp.max(m2, axis=1, keepdims=True)
    i2 = jnp.min(jnp.where(m2 == v2, lane_e, N_EXPERTS), axis=1, keepdims=True)
    e2 = jnp.exp(v2 - v1)
    p1 = 1.0 / (1.0 + e2)
    p2 = e2 / (1.0 + e2)
    return jnp.where(lane_e == i1, gw * p1, 0.0) + jnp.where(lane_e == i2, gw * p2, 0.0)


def _moe_kernel(x_ref, gain_ref, wrg_ref, wre_ref, wg_ref, wu_ref, wd_ref, o_ref, xn_sc, comb_sc, acc_sc, *, hi):
    e = pl.program_id(1)

    @pl.when(e == 0)
    def _():
        xn = _rmsnorm_rows(x_ref[...], gain_ref[...])
        comb_sc[...] = _route(xn, wrg_ref[...], wre_ref[...])
        xn_sc[...] = xn.astype(xn_sc.dtype)
        acc_sc[...] = jnp.zeros_like(acc_sc)

    xn = xn_sc[...]
    g = _dot(xn, wg_ref[0], hi)
    u = _dot(xn, wu_ref[0], hi)
    lane_e = lax.broadcasted_iota(jnp.int32, comb_sc.shape, 1)
    cw = jnp.sum(jnp.where(lane_e == e, comb_sc[...], 0.0), axis=1, keepdims=True)
    h = (g * _sigmoid(g)) * u * cw
    acc_sc[...] += _dot(h, wd_ref[0], hi)

    @pl.when(e == pl.num_programs(1) - 1)
    def _():
        o_ref[...] = x_ref[...] + acc_sc[...]


def _moe(x, gain, wrg, wre, wg, wu, wd, *, tile, hi):
    n = x.shape[0]
    assert n % tile == 0
    row_spec = pl.BlockSpec((tile, D_MODEL), lambda i, e: (i, 0))
    const2 = lambda shape: pl.BlockSpec(shape, lambda i, e: (0, 0))
    return pl.pallas_call(
        functools.partial(_moe_kernel, hi=hi),
        grid=(n // tile, N_EXPERTS),
        in_specs=[row_spec, const2((1, D_MODEL)), const2(wrg.shape), const2(wre.shape),
                  pl.BlockSpec((1, D_MODEL, D_EXPERT), lambda i, e: (e, 0, 0)),
                  pl.BlockSpec((1, D_MODEL, D_EXPERT), lambda i, e: (e, 0, 0)),
                  pl.BlockSpec((1, D_EXPERT, D_MODEL), lambda i, e: (e, 0, 0))],
        out_specs=row_spec,
        out_shape=jax.ShapeDtypeStruct((n, D_MODEL), F32),
        scratch_shapes=[pltpu.VMEM((tile, D_MODEL), F32 if hi else BF16),
                        pltpu.VMEM((tile, N_EXPERTS), F32),
                        pltpu.VMEM((tile, D_MODEL), F32)],
        compiler_params=_cparams(("arbitrary", "arbitrary"), 48),
        name="moe_hi" if hi else "moe",
    )(x, gain.reshape(1, -1), wrg, wre, wg, wu, wd)


def _head_ones():
    m = np.zeros((D_MODEL, D_MODEL), np.float32)
    for h in range(N_HEADS):
        m[h * HEAD_DIM:(h + 1) * HEAD_DIM, h * HEAD_DIM:(h + 1) * HEAD_DIM] = 1.0
    return m


def _head_rmsnorm(v, hh, gain, hi):
    ssq = _dot(v * v, hh, hi)
    return v * lax.rsqrt(ssq * (1.0 / HEAD_DIM) + EPS) * gain


def _proj_kernel(x_ref, kvn_ref, qn_ref, wk_ref, wv_ref, wf_ref, wft_ref, wq_ref, hh_ref, kg_ref, qg_ref,
                 bf_ref, bfc_ref, tri_ref, *out_and_scratch, hi, with_cum):
    if with_cum:
        k_ref, v_ref, lf_ref, qs_ref, kb_ref, vb_ref, cum_ref, carry = out_and_scratch
    else:
        k_ref, v_ref, lf_ref, qs_ref = out_and_scratch
    x = x_ref[...]
    ms = jnp.mean(x * x, axis=-1, keepdims=True)
    xhat = x * lax.rsqrt(ms + EPS)
    xn1 = xhat * kvn_ref[...]
    xn2 = xhat * qn_ref[...]
    hh = hh_ref[...]
    k = _head_rmsnorm(_dot(xn1, wk_ref[...], hi), hh, kg_ref[...], hi)
    v = _dot(xn1, wv_ref[...], hi)
    q = _head_rmsnorm(_dot(xn2, wq_ref[...], hi), hh, qg_ref[...], hi) * (HEAD_DIM ** -0.5)
    f = jnp.dot(xn1, wf_ref[...], precision=HI, preferred_element_type=F32)
    k_ref[...] = k
    v_ref[...] = v
    lf_ref[...] = _log_sigmoid(f + bf_ref[...])
    qs_ref[...] = q.astype(qs_ref.dtype)
    if with_cum:
        i = pl.program_id(0)

        @pl.when(i == 0)
        def _():
            carry[...] = jnp.zeros_like(carry)

        kb_ref[...] = k.astype(BF16)
        vb_ref[...] = v.astype(BF16)
        ft = lax.dot_general(wft_ref[...], xn1, (((1,), (1,)), ((), ())), precision=HI,
                             preferred_element_type=F32)
        lft = _log_sigmoid(ft + bfc_ref[...])
        cum = jnp.dot(lft, tri_ref[...], precision=HI, preferred_element_type=F32) + carry[:, 0:1]
        cum_ref[...] = cum
        carry[...] = jnp.broadcast_to(cum[:, cum.shape[1] - 1:], carry.shape)


def _proj(x, kv_norm, q_in_norm, wk, wv, wf, wq, k_norm, q_norm, b_forget, *, tile, hi, with_cum):
    n = x.shape[0]
    assert n % tile == 0
    wdt = F32 if hi else BF16
    row = lambda w: pl.BlockSpec((tile, w), lambda i: (i, 0))
    const = lambda shape: pl.BlockSpec(shape, lambda i: (0, 0))
    tri = np.triu(np.ones((tile, tile), np.float32))
    gain_tiled = lambda g: jnp.tile(g, N_HEADS).reshape(1, D_MODEL)
    out_specs = [row(D_MODEL), row(D_MODEL), row(N_HEADS), row(D_MODEL)]
    out_shape = [jax.ShapeDtypeStruct((n, D_MODEL), F32), jax.ShapeDtypeStruct((n, D_MODEL), F32),
                 jax.ShapeDtypeStruct((n, N_HEADS), F32), jax.ShapeDtypeStruct((n, D_MODEL), F32 if hi else BF16)]
    scratch = []
    if with_cum:
        out_specs += [row(D_MODEL), row(D_MODEL), pl.BlockSpec((N_HEADS, tile), lambda i: (0, i))]
        out_shape += [jax.ShapeDtypeStruct((n, D_MODEL), BF16), jax.ShapeDtypeStruct((n, D_MODEL), BF16),
                      jax.ShapeDtypeStruct((N_HEADS, n), F32)]
        scratch = [pltpu.VMEM((N_HEADS, LANES), F32)]
    return pl.pallas_call(
        functools.partial(_proj_kernel, hi=hi, with_cum=with_cum),
        grid=(n // tile,),
        in_specs=[row(D_MODEL), const((1, D_MODEL)), const((1, D_MODEL)),
                  const((D_MODEL, D_MODEL)), const((D_MODEL, D_MODEL)), const((D_MODEL, N_HEADS)),
                  const((N_HEADS, D_MODEL)), const((D_MODEL, D_MODEL)), const((D_MODEL, D_MODEL)),
                  const((1, D_MODEL)), const((1, D_MODEL)), const((1, N_HEADS)), const((N_HEADS, 1)),
                  const((tile, tile))],
        out_specs=out_specs, out_shape=out_shape, scratch_shapes=scratch,
        compiler_params=_cparams(("arbitrary",), 56),
        name="proj_hi" if hi else "proj",
    )(x, kv_norm.reshape(1, -1), q_in_norm.reshape(1, -1), wk.astype(wdt), wv.astype(wdt), wf, wf.T,
      wq.astype(wdt), jnp.asarray(_head_ones(), wdt), gain_tiled(k_norm), gain_tiled(q_norm),
      b_forget.reshape(1, -1), b_forget.reshape(-1, 1), jnp.asarray(tri))


def _attn_prompt_kernel(jstart_ref, q_ref, k_ref, v_ref, cum_ref, o_ref, m_sc, l_sc, acc_sc):
    hp = pl.program_id(0)
    qb = pl.program_id(1)
    blk = ATT_BLOCK
    q2 = q_ref[...]
    lane = lax.broadcasted_iota(jnp.int32, (blk, LANES), 1)
    first = lane < HEAD_DIM
    qh = (jnp.where(first, q2, jnp.zeros_like(q2)), jnp.where(first, jnp.zeros_like(q2), q2))
    m_sc[...] = jnp.full(m_sc.shape, NEG, F32)
    l_sc[...] = jnp.zeros_like(l_sc)
    acc_sc[...] = jnp.zeros_like(acc_sc)
    base = [cum_ref[0, h, qb, :, 0:1] for h in range(2)]
    row_i = lax.broadcasted_iota(jnp.int32, (blk, blk), 0)
    col_i = lax.broadcasted_iota(jnp.int32, (blk, blk), 1)

    def visit(j, diagonal):
        start = pl.multiple_of(j * blk, blk)
        kblk = k_ref[pl.ds(start, blk), :]
        vblk = v_ref[pl.ds(start, blk), :]
        for h in range(2):
            s = lax.dot_general(qh[h], kblk, (((1,), (1,)), ((), ())), preferred_element_type=F32)
            s = s - (cum_ref[0, h, j] - base[h])
            if diagonal:
                s = jnp.where(col_i <= row_i, s, NEG)
            m_old = m_sc[h]
            m_new = jnp.maximum(m_old, jnp.max(s, axis=1, keepdims=True))
            alpha = jnp.exp(m_old - m_new)
            p = jnp.exp(s - m_new[:, 0:1])
            l_sc[h] = alpha * l_sc[h] + jnp.sum(p, axis=1, keepdims=True)
            acc_sc[h] = alpha * acc_sc[h] + jnp.dot(p.astype(BF16), vblk, preferred_element_type=F32)
            m_sc[h] = m_new

    def body(j, carry):
        visit(j, False)
        return carry

    lax.fori_loop(jstart_ref[hp, qb], qb, body, 0)
    visit(qb, True)
    o = jnp.where(first, acc_sc[0] / l_sc[0], acc_sc[1] / l_sc[1])
    o_ref[...] = o.astype(o_ref.dtype)


def _first_key_block(cum_t, q_norm, k_norm):
    t = cum_t.shape[1]
    nb = t // ATT_BLOCK
    bound = 1.02 * HEAD_DIM ** 0.5 * jnp.max(jnp.abs(q_norm)) * jnp.max(jnp.abs(k_norm))
    limit = SKIP_LOGIT_GAP + 2.0 * bound
    c_first_q = cum_t[:, ::ATT_BLOCK]
    c_last_k = cum_t[:, ATT_BLOCK - 1::ATT_BLOCK]
    gap = c_first_q[:, :, None] - c_last_k[:, None, :]
    j_idx = jnp.arange(nb)
    skip = (gap < -limit) & (j_idx[None, None, :] < j_idx[None, :, None])
    first = jnp.sum(skip, axis=2).astype(jnp.int32)
    return jnp.min(first.reshape(N_HEADS // 2, 2, nb), axis=1)


def _attn_prompt(q, k, v, cum_t, jstart):
    t = q.shape[0]
    nb = t // ATT_BLOCK
    cum4 = cum_t.reshape(N_HEADS // 2, 2, nb, 1, ATT_BLOCK)
    grid_spec = pltpu.PrefetchScalarGridSpec(
        num_scalar_prefetch=1,
        grid=(N_HEADS // 2, nb),
        in_specs=[pl.BlockSpec((ATT_BLOCK, LANES), lambda hp, qb, js: (qb, hp)),
                  pl.BlockSpec((t, LANES), lambda hp, qb, js: (0, hp)),
                  pl.BlockSpec((t, LANES), lambda hp, qb, js: (0, hp)),
                  pl.BlockSpec((1, 2, nb, 1, ATT_BLOCK), lambda hp, qb, js: (hp, 0, 0, 0, 0))],
        out_specs=pl.BlockSpec((ATT_BLOCK, LANES), lambda hp, qb, js: (qb, hp)),
        scratch_shapes=[pltpu.VMEM((2, ATT_BLOCK, LANES), F32), pltpu.VMEM((2, ATT_BLOCK, LANES), F32),
                        pltpu.VMEM((2, ATT_BLOCK, LANES), F32)])
    return pl.pallas_call(
        _attn_prompt_kernel, grid_spec=grid_spec,
        out_shape=jax.ShapeDtypeStruct((t, D_MODEL), BF16),
        compiler_params=_cparams(("arbitrary", "arbitrary"), 48),
        name="attn_prompt",
    )(jstart, q, k, v, cum4)


def _head_select():
    m = np.zeros((D_MODEL, N_HEADS), np.float32)
    for h in range(N_HEADS):
        m[h * HEAD_DIM:(h + 1) * HEAD_DIM, h] = 1.0
    return m


def _attn_sample_kernel(pt_ref, q_ref, kc_ref, vc_ref, lfc_ref, kn_ref, vn_ref, lfn_ref, hd_ref, hdt_ref, tri_ref,
                        o_ref, m_sc, l_sc, acc_sc, cum_sc):
    del pt_ref
    p = pl.program_id(1)

    @pl.when(p == 0)
    def _():
        m_sc[...] = jnp.full(m_sc.shape, NEG, F32)
        l_sc[...] = jnp.zeros_like(l_sc)
        acc_sc[...] = jnp.zeros_like(acc_sc)
        cum_sc[...] = jnp.zeros_like(cum_sc)

    q = q_ref[0]
    hd = hd_ref[...]
    hdt = hdt_ref[...]
    rows8 = (SUBLANES, N_HEADS)

    def expand(a):
        return jnp.dot(jnp.broadcast_to(a, rows8), hdt, precision=HI, preferred_element_type=F32)[0:1]

    def update(s, vals):
        m_old = m_sc[0:1, :]
        m_new = jnp.maximum(m_old, jnp.max(s, axis=0, keepdims=True))
        alpha = jnp.exp(m_old - m_new)
        pr = jnp.exp(s - m_new)
        l_sc[...] = jnp.broadcast_to(alpha * l_sc[0:1, :] + jnp.sum(pr, axis=0, keepdims=True), l_sc.shape)
        m_sc[...] = jnp.broadcast_to(m_new, m_sc.shape)
        if pr.shape[0] == 1:
            pv = expand(pr) * vals
        else:
            pv = jnp.sum(jnp.dot(pr, hdt, precision=HI, preferred_element_type=F32) * vals, axis=0, keepdims=True)
        acc_sc[...] = jnp.broadcast_to(expand(alpha) * acc_sc[0:1, :] + pv, acc_sc.shape)

    kpage = kc_ref[0]
    s = jnp.dot(kpage * q, hd, precision=HI, preferred_element_type=F32)
    cum = jnp.dot(tri_ref[...], lfc_ref[0], precision=HI, preferred_element_type=F32) + cum_sc[0:1, :]
    update(s - cum, vc_ref[0])
    cum_sc[...] = jnp.broadcast_to(cum[cum.shape[0] - 1:, :], cum_sc.shape)

    @pl.when(p == pl.num_programs(1) - 1)
    def _():
        s_new = (jnp.dot(jnp.broadcast_to(kn_ref[0] * q, (SUBLANES, D_MODEL)), hd, precision=HI,
                         preferred_element_type=F32)[0:1] - (cum_sc[0:1, :] + lfn_ref[0]))
        update(s_new, vn_ref[0])
        o_ref[0] = acc_sc[0:1, :] / expand(l_sc[0:1, :])


def _attn_sample(q, cache_k, cache_v, cache_logf, page_table, k_new, v_new, lf_new):
    nseq, n_pages = page_table.shape
    page = cache_k.shape[1]
    tri = np.tril(np.ones((page, page), np.float32))
    hd = _head_select()
    seq3 = lambda a: a.reshape(nseq, 1, a.shape[-1])
    per_seq = lambda w: pl.BlockSpec((1, 1, w), lambda b, p, pt: (b, 0, 0))
    paged = lambda w: pl.BlockSpec((1, page, w), lambda b, p, pt: (pt[b, p], 0, 0))
    const = lambda shape: pl.BlockSpec(shape, lambda b, p, pt: (0, 0))
    grid_spec = pltpu.PrefetchScalarGridSpec(
        num_scalar_prefetch=1,
        grid=(nseq, n_pages),
        in_specs=[per_seq(D_MODEL), paged(D_MODEL), paged(D_MODEL), paged(N_HEADS),
                  per_seq(D_MODEL), per_seq(D_MODEL), per_seq(N_HEADS),
                  const((D_MODEL, N_HEADS)), const((N_HEADS, D_MODEL)), const((page, page))],
        out_specs=per_seq(D_MODEL),
        scratch_shapes=[pltpu.VMEM((SUBLANES, N_HEADS), F32), pltpu.VMEM((SUBLANES, N_HEADS), F32),
                        pltpu.VMEM((SUBLANES, D_MODEL), F32), pltpu.VMEM((SUBLANES, N_HEADS), F32)])
    out = pl.pallas_call(
        _attn_sample_kernel, grid_spec=grid_spec,
        out_shape=jax.ShapeDtypeStruct((nseq, 1, D_MODEL), F32),
        compiler_params=_cparams(("arbitrary", "arbitrary"), 32),
        name="attn_sample",
    )(page_table, seq3(q), cache_k.reshape(-1, page, D_MODEL), cache_v.reshape(-1, page, D_MODEL), cache_logf,
      seq3(k_new), seq3(v_new), seq3(lf_new), jnp.asarray(hd), jnp.asarray(hd.T), jnp.asarray(tri))
    return out.reshape(nseq, D_MODEL)


def _wo_kernel(x_ref, a_ref, w_ref, o_ref, *, hi):
    o_ref[...] = x_ref[...] + _dot(a_ref[...], w_ref[...], hi)


def _wo(x, attn, w, *, tile, hi):
    n = x.shape[0]
    row = pl.BlockSpec((tile, D_MODEL), lambda i: (i, 0))
    return pl.pallas_call(
        functools.partial(_wo_kernel, hi=hi),
        grid=(n // tile,),
        in_specs=[row, row, pl.BlockSpec((D_MODEL, D_MODEL), lambda i: (0, 0))],
        out_specs=row,
        out_shape=jax.ShapeDtypeStruct((n, D_MODEL), F32),
        compiler_params=_cparams(("arbitrary",), 32),
        name="wo_hi" if hi else "wo",
    )(x, attn, w if hi else w.astype(BF16))


MOE_TILE = 1024
PROJ_TILE = 256
WO_TILE = 512


def kernel(x_prompt, x_sample, state_ssm_re, state_ssm_im, cache_k, cache_v, cache_logf, page_table, norm_mix, norm_ffn, ssm_lambda_re, ssm_lambda_im, ssm_log_dt, ssm_b_re, ssm_b_im, ssm_c_re, ssm_c_im, ssm_d, ssm_w_glu, ssm_b_glu, kv_norm, w_kvf, b_forget, k_norm, w_q, q_norm, w_o, w_router_group, w_router_expert, w_gate, w_up, w_down):
    t = x_prompt.shape[1]
    nseq = x_sample.shape[0]
    hd = N_HEADS * HEAD_DIM
    xp = x_prompt.reshape(t, D_MODEL)
    xs = x_sample.reshape(nseq, D_MODEL)

    apow_re, apow_im, bb_re, bb_im = _ssm_prep(ssm_lambda_re[0], ssm_lambda_im[0], ssm_log_dt[0],
                                               ssm_b_re[0], ssm_b_im[0])
    bmat, cmat, consts, a_rows = _ssm_layouts(apow_re, apow_im, bb_re, bb_im, ssm_c_re[0], ssm_c_im[0])
    wk, wv, wf = w_kvf[:, :hd], w_kvf[:, hd:2 * hd], w_kvf[:, 2 * hd:]

    def moe(x, layer, tile, hi):
        return _moe(x, norm_ffn[layer], w_router_group[layer], w_router_expert[layer],
                    w_gate[layer], w_up[layer], w_down[layer], tile=tile, hi=hi)

    x1, s_end = _s5_prompt(xp, norm_mix[0], bmat, cmat, consts, ssm_d[0], ssm_w_glu[0], ssm_b_glu[0])
    x2 = moe(x1, 0, MOE_TILE, False)
    k_p, v_p, lf_p, q_p, kb_p, vb_p, cum_t = _proj(x2, kv_norm, norm_mix[1], wk, wv, wf, w_q[0], k_norm, q_norm[0],
                                                  b_forget, tile=PROJ_TILE, hi=False, with_cum=True)
    jstart = _first_key_block(cum_t, q_norm[0], k_norm)
    attn_p = _attn_prompt(q_p, kb_p, vb_p, cum_t, jstart)
    x3 = _wo(x2, attn_p, w_o[0], tile=WO_TILE, hi=False)
    y_p = moe(x3, 1, MOE_TILE, False)
    sre_p, sim_p = _state_from_layout(s_end)

    s0 = _state_to_layout(state_ssm_re[0], state_ssm_im[0])
    x1s, s_new = _s5_sample(xs, norm_mix[0], bmat, cmat, a_rows, s0, ssm_d[0], ssm_w_glu[0], ssm_b_glu[0])
    x2s = moe(x1s, 0, nseq, True)
    k_s, v_s, lf_s, q_s = _proj(x2s, kv_norm, norm_mix[1], wk, wv, wf, w_q[0], k_norm, q_norm[0], b_forget,
                                tile=nseq, hi=True, with_cum=False)
    attn_s = _attn_sample(q_s, cache_k, cache_v, cache_logf, page_table, k_s, v_s, lf_s)
    x3s = _wo(x2s, attn_s, w_o[0], tile=nseq, hi=True)
    y_s = moe(x3s, 1, nseq, True)
    sre_s, sim_s = _state_from_layout(s_new)

    return (y_p.reshape(1, t, D_MODEL), y_s.reshape(nseq, 1, D_MODEL),
            sre_p[None], sim_p[None],
            k_p.reshape(1, t, N_HEADS, HEAD_DIM), v_p.reshape(1, t, N_HEADS, HEAD_DIM), lf_p.reshape(1, t, N_HEADS),
            sre_s[None], sim_s[None],
            k_s.reshape(nseq, 1, N_HEADS, HEAD_DIM), v_s.reshape(nseq, 1, N_HEADS, HEAD_DIM),
            lf_s.reshape(nseq, 1, N_HEADS))
```

```python
import functools

import numpy as np
import jax
import jax.numpy as jnp
from jax import lax
from jax.experimental import pallas as pl
from jax.experimental.pallas import tpu as pltpu

F32 = jnp.float32
BF16 = jnp.bfloat16
HI = lax.Precision.HIGHEST

D_MODEL = 1024
SSM_GROUP = 16
SSM_GROUPS = 64
SSM_STATE = 64
N_HEADS = 16
HEAD_DIM = 64
N_EXPERT_GROUPS = 4
EXPERTS_PER_GROUP = 8
N_EXPERTS = 32
D_EXPERT = 256
EPS = 1e-6
NEG = -1e30

V7X_VMEM_BYTES = 64 * 1024 * 1024
LANES = 128
SUBLANES = 8

SEG_COUNT = SUBLANES
SEG_LEN = 32
S5_TILE = SEG_COUNT * SEG_LEN
S5_BLOCKS = 4
S5_BLOCK_CH = D_MODEL // S5_BLOCKS
S5_BLOCK_ST = 2 * SSM_GROUPS * SSM_STATE // S5_BLOCKS
S5_CHUNK = 512
S5_NCHUNK = 2 * SSM_GROUPS * SSM_STATE // S5_CHUNK
S5_NPAIR = S5_NCHUNK // 2
S5_EXPS = (1, 0) + tuple(SEG_LEN * q for q in range(1, SEG_COUNT))

ATT_BLOCK = 256
SKIP_LOGIT_GAP = 100.0


def _cparams(semantics, vmem_mb):
    return pltpu.CompilerParams(dimension_semantics=semantics,
                                vmem_limit_bytes=min(vmem_mb * 1024 * 1024, V7X_VMEM_BYTES - 4 * 1024 * 1024))


def _dot(a, b, hi):
    if hi:
        return jnp.dot(a, b, precision=HI, preferred_element_type=F32)
    return jnp.dot(a.astype(BF16), b.astype(BF16), preferred_element_type=F32)


def _sigmoid(x):
    return 1.0 / (1.0 + jnp.exp(-x))


def _log_sigmoid(x):
    return jnp.minimum(x, 0.0) - jnp.log(1.0 + jnp.exp(-jnp.abs(x)))


def _gelu_tanh(x):
    c = np.float32(np.sqrt(2.0 / np.pi))
    return 0.5 * x * (1.0 + jnp.tanh(c * (x + 0.044715 * (x * x * x))))


def _rmsnorm_rows(x, gain):
    ms = jnp.mean(x * x, axis=-1, keepdims=True)
    return x * lax.rsqrt(ms + EPS) * gain


def _ssm_prep_kernel(lam_re_ref, lam_im_ref, log_dt_ref, lam_re_x_ref, lam_im_x_ref, b_re_ref, b_im_ref,
                     apow_re_ref, apow_im_ref, bb_re_ref, bb_im_ref):
    dt = jnp.exp(log_dt_ref[...])

    def disc(lr, li):
        mag = jnp.exp(lr * dt)
        ang = li * dt
        return mag * jnp.cos(ang), mag * jnp.sin(ang)

    def cmul(a, b):
        return a[0] * b[0] - a[1] * b[1], a[0] * b[1] + a[1] * b[0]

    a1 = disc(lam_re_ref[...], lam_im_ref[...])
    a_seg = a1
    for _ in range(int(np.log2(SEG_LEN))):
        a_seg = cmul(a_seg, a_seg)
    powers = {1: a1, 0: (jnp.ones_like(a1[0]), jnp.zeros_like(a1[0])), SEG_LEN: a_seg}
    for q in range(2, SEG_COUNT):
        powers[SEG_LEN * q] = cmul(powers[SEG_LEN * (q - 1)], a_seg)
    for i, e in enumerate(S5_EXPS):
        apow_re_ref[i] = powers[e][0]
        apow_im_ref[i] = powers[e][1]

    lrx, lix = lam_re_x_ref[...], lam_im_x_ref[...]
    arx, aix = disc(lrx, lix)
    den = lrx * lrx + lix * lix
    nr, ni = arx - 1.0, aix
    wr = (nr * lrx + ni * lix) / den
    wi = (ni * lrx - nr * lix) / den
    br, bi = b_re_ref[...], b_im_ref[...]
    bb_re_ref[...] = wr * br - wi * bi
    bb_im_ref[...] = wr * bi + wi * br


def _ssm_prep(lam_re, lam_im, log_dt, b_re, b_im):
    g, p, h = SSM_GROUPS, SSM_STATE, SSM_GROUP
    rep = lambda a: jnp.repeat(a, h, axis=1)
    n_exp = len(S5_EXPS)
    out_shape = (jax.ShapeDtypeStruct((n_exp, g, p), F32), jax.ShapeDtypeStruct((n_exp, g, p), F32),
                 jax.ShapeDtypeStruct((g, p * h), F32), jax.ShapeDtypeStruct((g, p * h), F32))
    apow_re, apow_im, bb_re, bb_im = pl.pallas_call(
        _ssm_prep_kernel, out_shape=out_shape, name="ssm_prep",
    )(lam_re, lam_im, log_dt.reshape(g, 1), rep(lam_re), rep(lam_im),
      b_re.reshape(g, p * h), b_im.reshape(g, p * h))
    return apow_re, apow_im, bb_re.reshape(g, p, h), bb_im.reshape(g, p, h)


def _ssm_layouts(apow_re, apow_im, bb_re, bb_im, c_re, c_im):
    nb, gl = S5_BLOCKS, SSM_GROUPS // S5_BLOCKS
    eye = jnp.eye(gl, dtype=F32)

    def embed_b(bb):
        v = bb.transpose(0, 2, 1).reshape(nb, gl, SSM_GROUP, SSM_STATE)
        return jnp.einsum('jghp,gk->jghkp', v, eye).reshape(nb, gl * SSM_GROUP, gl * SSM_STATE)

    def embed_c(c):
        v = c.transpose(0, 2, 1).reshape(nb, gl, SSM_STATE, SSM_GROUP)
        return jnp.einsum('jgph,gk->jgpkh', v, eye).reshape(nb, gl * SSM_STATE, gl * SSM_GROUP)

    bmat = jnp.concatenate([embed_b(bb_re), embed_b(bb_im)], axis=2)
    cmat = jnp.concatenate([embed_c(c_re), -embed_c(c_im)], axis=1)

    flat = lambda a: a.reshape(a.shape[0], S5_NPAIR, S5_CHUNK)
    pr, pi = flat(apow_re), flat(apow_im)
    idx = {e: i for i, e in enumerate(S5_EXPS)}
    rows = [pr[idx[1]], pi[idx[1]]]
    for d in (1, 2, 4):
        rows += [pr[idx[SEG_LEN * d]], pi[idx[SEG_LEN * d]]]
    rows += [pr[idx[SEG_LEN * q]] for q in range(SEG_COUNT)]
    rows += [pi[idx[SEG_LEN * q]] for q in range(SEG_COUNT)]
    consts = jnp.stack(rows, axis=1)
    a_rows = jnp.stack([apow_re[idx[1]].reshape(-1), apow_im[idx[1]].reshape(-1)])
    return bmat, cmat, consts, a_rows


def _state_to_layout(s_re, s_im):
    b = s_re.shape[0]
    r = s_re.reshape(b, S5_BLOCKS, 1, S5_BLOCK_ST // 2)
    i = s_im.reshape(b, S5_BLOCKS, 1, S5_BLOCK_ST // 2)
    return jnp.concatenate([r, i], axis=2).reshape(b, S5_BLOCKS * S5_BLOCK_ST)


def _state_from_layout(s):
    b = s.shape[0]
    v = s.reshape(b, S5_BLOCKS, 2, S5_BLOCK_ST // 2)
    return (v[:, :, 0].reshape(b, SSM_GROUPS, SSM_STATE), v[:, :, 1].reshape(b, SSM_GROUPS, SSM_STATE))


def _segment_permutation():
    m = np.zeros((S5_TILE, S5_TILE), np.float32)
    for rp in range(S5_TILE):
        k, q = divmod(rp, SEG_COUNT)
        m[rp, q * SEG_LEN + k] = 1.0
    return m


def _s5_prompt_kernel(x_ref, gain_ref, perm_ref, perm_t_ref, bmat_ref, cmat_ref, const_ref, dskip_ref,
                      wglu_ref, bglu_ref, o_ref, state_ref, pbuf, sbuf, carry):
    i = pl.program_id(0)

    @pl.when(i == 0)
    def _():
        carry[...] = jnp.zeros_like(carry)

    x = x_ref[...]
    u = _rmsnorm_rows(x, gain_ref[...])
    up = jnp.dot(perm_ref[...], u.astype(BF16), preferred_element_type=F32).astype(BF16)
    for j in range(S5_BLOCKS):
        pj = jnp.dot(up[:, j * S5_BLOCK_CH:(j + 1) * S5_BLOCK_CH], bmat_ref[j], preferred_element_type=F32)
        for c in range(S5_BLOCK_ST // S5_CHUNK):
            pbuf[j * (S5_BLOCK_ST // S5_CHUNK) + c] = pj[:, c * S5_CHUNK:(c + 1) * S5_CHUNK]

    row = lax.broadcasted_iota(jnp.int32, (SEG_COUNT, S5_CHUNK), 0)

    def chunk_pair(cp, _):
        c_re = (cp // 2) * 4 + (cp % 2)
        c_im = c_re + 2
        cst = const_ref[cp]
        ar = jnp.broadcast_to(cst[0:1], (SEG_COUNT, S5_CHUNK))
        ai = jnp.broadcast_to(cst[1:2], (SEG_COUNT, S5_CHUNK))

        def step(k, sr, si):
            pr = pbuf[c_re, k * SEG_COUNT:(k + 1) * SEG_COUNT, :]
            pi = pbuf[c_im, k * SEG_COUNT:(k + 1) * SEG_COUNT, :]
            return ar * sr - ai * si + pr, ar * si + ai * sr + pi

        er = jnp.zeros((SEG_COUNT, S5_CHUNK), F32)
        ei = jnp.zeros((SEG_COUNT, S5_CHUNK), F32)
        for k in range(SEG_LEN):
            er, ei = step(k, er, ei)

        for n, d in enumerate((1, 2, 4)):
            cr, ci = cst[2 + 2 * n:3 + 2 * n], cst[3 + 2 * n:4 + 2 * n]
            shr = jnp.where(row >= d, pltpu.roll(er, d, 0), 0.0)
            shi = jnp.where(row >= d, pltpu.roll(ei, d, 0), 0.0)
            er, ei = er + cr * shr - ci * shi, ei + cr * shi + ci * shr
        exr = jnp.where(row >= 1, pltpu.roll(er, 1, 0), 0.0)
        exi = jnp.where(row >= 1, pltpu.roll(ei, 1, 0), 0.0)
        c0r = carry[c_re, SEG_COUNT - 1:SEG_COUNT, :]
        c0i = carry[c_im, SEG_COUNT - 1:SEG_COUNT, :]
        qr, qi = cst[8:16], cst[16:24]
        sr = exr + qr * c0r - qi * c0i
        si = exi + qr * c0i + qi * c0r

        for k2 in range(SEG_LEN // 2):
            sr1, si1 = step(2 * k2, sr, si)
            sr, si = step(2 * k2 + 1, sr1, si1)
            rows = slice(2 * k2 * SEG_COUNT, (2 * k2 + 2) * SEG_COUNT)
            sbuf[c_re, rows, :] = jnp.concatenate([sr1, sr], axis=0).astype(BF16)
            sbuf[c_im, rows, :] = jnp.concatenate([si1, si], axis=0).astype(BF16)
        carry[c_re] = sr
        carry[c_im] = si
        return 0

    lax.fori_loop(0, S5_NPAIR, chunk_pair, 0)

    ys = []
    for j in range(S5_BLOCKS):
        acc = None
        for c in range(S5_BLOCK_ST // S5_CHUNK):
            t = jnp.dot(sbuf[j * (S5_BLOCK_ST // S5_CHUNK) + c], cmat_ref[j, c * S5_CHUNK:(c + 1) * S5_CHUNK, :],
                        preferred_element_type=F32)
            acc = t if acc is None else acc + t
        ys.append(acc)
    yp = jnp.concatenate(ys, axis=1)
    yh = yp.astype(BF16)
    yl = (yp - yh.astype(F32)).astype(BF16)
    pt = perm_t_ref[...]
    y = (jnp.dot(pt, yh, preferred_element_type=F32) + jnp.dot(pt, yl, preferred_element_type=F32)
         + dskip_ref[...] * u)
    z = _gelu_tanh(y)
    zz = jnp.dot(z.astype(BF16), wglu_ref[...], preferred_element_type=F32) + bglu_ref[...]
    o_ref[...] = x + zz[:, :D_MODEL] * _sigmoid(zz[:, D_MODEL:])

    @pl.when(i == pl.num_programs(0) - 1)
    def _():
        state_ref[...] = carry[...]


def _s5_prompt(x, gain, bmat, cmat, consts, dskip, wglu, bglu):
    t = x.shape[0]
    assert t % S5_TILE == 0
    perm = _segment_permutation()
    const_spec = lambda shape: pl.BlockSpec(shape, lambda i: (0,) * len(shape))
    y, state = pl.pallas_call(
        _s5_prompt_kernel,
        grid=(t // S5_TILE,),
        in_specs=[pl.BlockSpec((S5_TILE, D_MODEL), lambda i: (i, 0)),
                  const_spec((1, D_MODEL)),
                  const_spec((S5_TILE, S5_TILE)), const_spec((S5_TILE, S5_TILE)),
                  const_spec(bmat.shape), const_spec(cmat.shape), const_spec(consts.shape),
                  const_spec((1, D_MODEL)),
                  const_spec((D_MODEL, 2 * D_MODEL)), const_spec((1, 2 * D_MODEL))],
        out_specs=[pl.BlockSpec((S5_TILE, D_MODEL), lambda i: (i, 0)),
                   const_spec((S5_NCHUNK, SEG_COUNT, S5_CHUNK))],
        out_shape=[jax.ShapeDtypeStruct((t, D_MODEL), F32),
                   jax.ShapeDtypeStruct((S5_NCHUNK, SEG_COUNT, S5_CHUNK), F32)],
        scratch_shapes=[pltpu.VMEM((S5_NCHUNK, S5_TILE, S5_CHUNK), F32),
                        pltpu.VMEM((S5_NCHUNK, S5_TILE, S5_CHUNK), BF16),
                        pltpu.VMEM((S5_NCHUNK, SEG_COUNT, S5_CHUNK), F32)],
        compiler_params=_cparams(("arbitrary",), 56),
        name="s5_prompt",
    )(x, gain.reshape(1, -1), jnp.asarray(perm, BF16), jnp.asarray(perm.T, BF16),
      bmat.astype(BF16), cmat.astype(BF16), consts, dskip.reshape(1, -1), wglu.astype(BF16), bglu.reshape(1, -1))
    s = state[:, SEG_COUNT - 1, :].reshape(1, S5_BLOCKS * S5_BLOCK_ST)
    return y, s


def _s5_sample_kernel(x_ref, gain_ref, bmat_ref, cmat_ref, a_ref, s0_ref, dskip_ref, wglu_ref, bglu_ref,
                      o_ref, s_ref):
    x = x_ref[...]
    u = _rmsnorm_rows(x, gain_ref[...])
    half = S5_BLOCK_ST // 2
    ys = []
    for j in range(S5_BLOCKS):
        pj = jnp.dot(u[:, j * S5_BLOCK_CH:(j + 1) * S5_BLOCK_CH], bmat_ref[j], precision=HI,
                     preferred_element_type=F32)
        ar = a_ref[0:1, j * half:(j + 1) * half]
        ai = a_ref[1:2, j * half:(j + 1) * half]
        s0r = s0_ref[:, j * S5_BLOCK_ST:j * S5_BLOCK_ST + half]
        s0i = s0_ref[:, j * S5_BLOCK_ST + half:(j + 1) * S5_BLOCK_ST]
        sr = ar * s0r - ai * s0i + pj[:, :half]
        si = ar * s0i + ai * s0r + pj[:, half:]
        s_ref[:, j * S5_BLOCK_ST:j * S5_BLOCK_ST + half] = sr
        s_ref[:, j * S5_BLOCK_ST + half:(j + 1) * S5_BLOCK_ST] = si
        ys.append(jnp.dot(sr, cmat_ref[j, :half, :], precision=HI, preferred_element_type=F32)
                  + jnp.dot(si, cmat_ref[j, half:, :], precision=HI, preferred_element_type=F32))
    y = jnp.concatenate(ys, axis=1) + dskip_ref[...] * u
    z = _gelu_tanh(y)
    zz = jnp.dot(z, wglu_ref[...], precision=HI, preferred_element_type=F32) + bglu_ref[...]
    o_ref[...] = x + zz[:, :D_MODEL] * _sigmoid(zz[:, D_MODEL:])


def _s5_sample(x, gain, bmat, cmat, a_rows, s0, dskip, wglu, bglu):
    n = x.shape[0]
    return pl.pallas_call(
        _s5_sample_kernel,
        out_shape=[jax.ShapeDtypeStruct((n, D_MODEL), F32), jax.ShapeDtypeStruct(s0.shape, F32)],
        compiler_params=_cparams(None, 56),
        name="s5_sample",
    )(x, gain.reshape(1, -1), bmat, cmat, a_rows, s0, dskip.reshape(1, -1), wglu, bglu.reshape(1, -1))


def _route(xn, wrg, wre):
    gl = jnp.dot(xn, wrg, precision=HI, preferred_element_type=F32)
    el = jnp.dot(xn, wre, precision=HI, preferred_element_type=F32)
    rows = xn.shape[0]
    lane_g = lax.broadcasted_iota(jnp.int32, (rows, N_EXPERT_GROUPS), 1)
    gmax = jnp.max(gl, axis=1, keepdims=True)
    gidx = jnp.min(jnp.where(gl == gmax, lane_g, N_EXPERT_GROUPS), axis=1, keepdims=True)
    gw = 1.0 / jnp.sum(jnp.exp(gl - gmax), axis=1, keepdims=True)
    lane_e = lax.broadcasted_iota(jnp.int32, (rows, N_EXPERTS), 1)
    in_group = lax.shift_right_logical(lane_e, 3) == gidx
    m1 = jnp.where(in_group, el, -jnp.inf)
    v1 = jnp.max(m1, axis=1, keepdims=True)
    i1 = jnp.min(jnp.where(m1 == v1, lane_e, N_EXPERTS), axis=1, keepdims=True)
    m2 = jnp.where(lane_e == i1, -jnp.inf, m1)
    v2 = jnp.max(m2, axis=1, keepdims=True)
    i2 = jnp.min(jnp.where(m2 == v2, lane_e, N_EXPERTS), axis=1, keepdims=True)
    e2 = jnp.exp(v2 - v1)
    p1 = 1.0 / (1.0 + e2)
    p2 = e2 / (1.0 + e2)
    return jnp.where(lane_e == i1, gw * p1, 0.0) + jnp.where(lane_e == i2, gw * p2, 0.0)


def _moe_kernel(x_ref, gain_ref, wrg_ref, wre_ref, wg_ref, wu_ref, wd_ref, o_ref, xn_sc, comb_sc, acc_sc, *, hi):
    e = pl.program_id(1)

    @pl.when(e == 0)
    def _():
        xn = _rmsnorm_rows(x_ref[...], gain_ref[...])
        comb_sc[...] = _route(xn, wrg_ref[...], wre_ref[...])
        xn_sc[...] = xn.astype(xn_sc.dtype)
        acc_sc[...] = jnp.zeros_like(acc_sc)

    xn = xn_sc[...]
    g = _dot(xn, wg_ref[0], hi)
    u = _dot(xn, wu_ref[0], hi)
    lane_e = lax.broadcasted_iota(jnp.int32, comb_sc.shape, 1)
    cw = jnp.sum(jnp.where(lane_e == e, comb_sc[...], 0.0), axis=1, keepdims=True)
    h = (g * _sigmoid(g)) * u * cw
    acc_sc[...] += _dot(h, wd_ref[0], hi)

    @pl.when(e == pl.num_programs(1) - 1)
    def _():
        o_ref[...] = x_ref[...] + acc_sc[...]


def _moe(x, gain, wrg, wre, wg, wu, wd, *, layer, tile, hi):
    n = x.shape[0]
    assert n % tile == 0
    row_spec = pl.BlockSpec((tile, D_MODEL), lambda i, e: (i, 0))
    const2 = lambda shape: pl.BlockSpec(shape, lambda i, e: (0, 0))
    return pl.pallas_call(
        functools.partial(_moe_kernel, hi=hi),
        grid=(n // tile, N_EXPERTS),
        in_specs=[row_spec, const2((1, D_MODEL)), const2(wrg.shape), const2(wre.shape),
                  pl.BlockSpec((None, 1, D_MODEL, D_EXPERT), lambda i, e: (layer, e, 0, 0)),
                  pl.BlockSpec((None, 1, D_MODEL, D_EXPERT), lambda i, e: (layer, e, 0, 0)),
                  pl.BlockSpec((None, 1, D_EXPERT, D_MODEL), lambda i, e: (layer, e, 0, 0))],
        out_specs=row_spec,
        out_shape=jax.ShapeDtypeStruct((n, D_MODEL), F32),
        scratch_shapes=[pltpu.VMEM((tile, D_MODEL), F32 if hi else BF16),
                        pltpu.VMEM((tile, N_EXPERTS), F32),
                        pltpu.VMEM((tile, D_MODEL), F32)],
        compiler_params=_cparams(("arbitrary", "arbitrary"), 48),
        name="moe_hi" if hi else "moe",
    )(x, gain.reshape(1, -1), wrg, wre, wg, wu, wd)


def _head_ones():
    m = np.zeros((D_MODEL, D_MODEL), np.float32)
    for h in range(N_HEADS):
        m[h * HEAD_DIM:(h + 1) * HEAD_DIM, h * HEAD_DIM:(h + 1) * HEAD_DIM] = 1.0
    return m


def _head_rmsnorm(v, hh, gain, hi):
    ssq = _dot(v * v, hh, hi)
    return v * lax.rsqrt(ssq * (1.0 / HEAD_DIM) + EPS) * gain


def _proj_kernel(x_ref, kvn_ref, qn_ref, wk_ref, wv_ref, wf_ref, wft_ref, wq_ref, hh_ref, kg_ref, qg_ref,
                 bf_ref, bfc_ref, tri_ref, *out_and_scratch, hi, with_cum):
    if with_cum:
        k_ref, v_ref, lf_ref, qs_ref, kb_ref, vb_ref, cum_ref, carry = out_and_scratch
    else:
        k_ref, v_ref, lf_ref, qs_ref = out_and_scratch
    x = x_ref[...]
    ms = jnp.mean(x * x, axis=-1, keepdims=True)
    xhat = x * lax.rsqrt(ms + EPS)
    xn1 = xhat * kvn_ref[...]
    xn2 = xhat * qn_ref[...]
    hh = hh_ref[...]
    k = _head_rmsnorm(_dot(xn1, wk_ref[...], hi), hh, kg_ref[...], hi)
    v = _dot(xn1, wv_ref[...], hi)
    q = _head_rmsnorm(_dot(xn2, wq_ref[...], hi), hh, qg_ref[...], hi) * (HEAD_DIM ** -0.5)
    f = jnp.dot(xn1, wf_ref[...], precision=HI, preferred_element_type=F32)
    k_ref[...] = k
    v_ref[...] = v
    lf_ref[...] = _log_sigmoid(f + bf_ref[...])
    qs_ref[...] = q.astype(qs_ref.dtype)
    if with_cum:
        i = pl.program_id(0)

        @pl.when(i == 0)
        def _():
            carry[...] = jnp.zeros_like(carry)

        kb_ref[...] = k.astype(BF16)
        vb_ref[...] = v.astype(BF16)
        ft = lax.dot_general(wft_ref[...], xn1, (((1,), (1,)), ((), ())), precision=HI,
                             preferred_element_type=F32)
        lft = _log_sigmoid(ft + bfc_ref[...])
        cum = jnp.dot(lft, tri_ref[...], precision=HI, preferred_element_type=F32) + carry[:, 0:1]
        cum_ref[...] = cum
        carry[...] = jnp.broadcast_to(cum[:, cum.shape[1] - 1:], carry.shape)


def _proj(x, kv_norm, q_in_norm, wk, wv, wf, wq, k_norm, q_norm, b_forget, *, tile, hi, with_cum):
    n = x.shape[0]
    assert n % tile == 0
    wdt = F32 if hi else BF16
    row = lambda w: pl.BlockSpec((tile, w), lambda i: (i, 0))
    const = lambda shape: pl.BlockSpec(shape, lambda i: (0, 0))
    tri = np.triu(np.ones((tile, tile), np.float32))
    gain_tiled = lambda g: jnp.tile(g, N_HEADS).reshape(1, D_MODEL)
    out_specs = [row(D_MODEL), row(D_MODEL), row(N_HEADS), row(D_MODEL)]
    out_shape = [jax.ShapeDtypeStruct((n, D_MODEL), F32), jax.ShapeDtypeStruct((n, D_MODEL), F32),
                 jax.ShapeDtypeStruct((n, N_HEADS), F32), jax.ShapeDtypeStruct((n, D_MODEL), F32 if hi else BF16)]
    scratch = []
    if with_cum:
        out_specs += [row(D_MODEL), row(D_MODEL), pl.BlockSpec((N_HEADS, tile), lambda i: (0, i))]
        out_shape += [jax.ShapeDtypeStruct((n, D_MODEL), BF16), jax.ShapeDtypeStruct((n, D_MODEL), BF16),
                      jax.ShapeDtypeStruct((N_HEADS, n), F32)]
        scratch = [pltpu.VMEM((N_HEADS, LANES), F32)]
    return pl.pallas_call(
        functools.partial(_proj_kernel, hi=hi, with_cum=with_cum),
        grid=(n // tile,),
        in_specs=[row(D_MODEL), const((1, D_MODEL)), const((1, D_MODEL)),
                  const((D_MODEL, D_MODEL)), const((D_MODEL, D_MODEL)), const((D_MODEL, N_HEADS)),
                  const((N_HEADS, D_MODEL)), const((D_MODEL, D_MODEL)), const((D_MODEL, D_MODEL)),
                  const((1, D_MODEL)), const((1, D_MODEL)), const((1, N_HEADS)), const((N_HEADS, 1)),
                  const((tile, tile))],
        out_specs=out_specs, out_shape=out_shape, scratch_shapes=scratch,
        compiler_params=_cparams(("arbitrary",), 56),
        name="proj_hi" if hi else "proj",
    )(x, kv_norm.reshape(1, -1), q_in_norm.reshape(1, -1), wk.astype(wdt), wv.astype(wdt), wf, wf.T,
      wq.astype(wdt), jnp.asarray(_head_ones(), wdt), gain_tiled(k_norm), gain_tiled(q_norm),
      b_forget.reshape(1, -1), b_forget.reshape(-1, 1), jnp.asarray(tri))


def _attn_prompt_kernel(jstart_ref, q_ref, k_ref, v_ref, cum_ref, o_ref, m_sc, l_sc, acc_sc):
    hp = pl.program_id(0)
    qb = pl.program_id(1)
    blk = ATT_BLOCK
    q2 = q_ref[...]
    lane = lax.broadcasted_iota(jnp.int32, (blk, LANES), 1)
    first = lane < HEAD_DIM
    qh = (jnp.where(first, q2, jnp.zeros_like(q2)), jnp.where(first, jnp.zeros_like(q2), q2))
    m_sc[...] = jnp.full(m_sc.shape, NEG, F32)
    l_sc[...] = jnp.zeros_like(l_sc)
    acc_sc[...] = jnp.zeros_like(acc_sc)
    base = [cum_ref[0, h, qb, :, 0:1] for h in range(2)]
    row_i = lax.broadcasted_iota(jnp.int32, (blk, blk), 0)
    col_i = lax.broadcasted_iota(jnp.int32, (blk, blk), 1)

    def visit(j, diagonal):
        start = pl.multiple_of(j * blk, blk)
        kblk = k_ref[pl.ds(start, blk), :]
        vblk = v_ref[pl.ds(start, blk), :]
        for h in range(2):
            s = lax.dot_general(qh[h], kblk, (((1,), (1,)), ((), ())), preferred_element_type=F32)
            s = s - (cum_ref[0, h, j] - base[h])
            if diagonal:
                s = jnp.where(col_i <= row_i, s, NEG)
            m_old = m_sc[h]
            m_new = jnp.maximum(m_old, jnp.max(s, axis=1, keepdims=True))
            alpha = jnp.exp(m_old - m_new)
            p = jnp.exp(s - m_new[:, 0:1])
            l_sc[h] = alpha * l_sc[h] + jnp.sum(p, axis=1, keepdims=True)
            acc_sc[h] = alpha * acc_sc[h] + jnp.dot(p.astype(BF16), vblk, preferred_element_type=F32)
            m_sc[h] = m_new

    def body(j, carry):
        visit(j, False)
        return carry

    lax.fori_loop(jstart_ref[hp, qb], qb, body, 0)
    visit(qb, True)
    o = jnp.where(first, acc_sc[0] / l_sc[0], acc_sc[1] / l_sc[1])
    o_ref[...] = o.astype(o_ref.dtype)


def _first_key_block(cum_t, q_norm, k_norm):
    t = cum_t.shape[1]
    nb = t // ATT_BLOCK
    bound = 1.02 * HEAD_DIM ** 0.5 * jnp.max(jnp.abs(q_norm)) * jnp.max(jnp.abs(k_norm))
    limit = SKIP_LOGIT_GAP + 2.0 * bound
    c_first_q = cum_t[:, ::ATT_BLOCK]
    c_last_k = cum_t[:, ATT_BLOCK - 1::ATT_BLOCK]
    gap = c_first_q[:, :, None] - c_last_k[:, None, :]
    j_idx = jnp.arange(nb)
    skip = (gap < -limit) & (j_idx[None, None, :] < j_idx[None, :, None])
    first = jnp.sum(skip, axis=2).astype(jnp.int32)
    return jnp.min(first.reshape(N_HEADS // 2, 2, nb), axis=1)


def _attn_prompt(q, k, v, cum_t, jstart):
    t = q.shape[0]
    nb = t // ATT_BLOCK
    cum4 = cum_t.reshape(N_HEADS // 2, 2, nb, 1, ATT_BLOCK)
    grid_spec = pltpu.PrefetchScalarGridSpec(
        num_scalar_prefetch=1,
        grid=(N_HEADS // 2, nb),
        in_specs=[pl.BlockSpec((ATT_BLOCK, LANES), lambda hp, qb, js: (qb, hp)),
                  pl.BlockSpec((t, LANES), lambda hp, qb, js: (0, hp)),
                  pl.BlockSpec((t, LANES), lambda hp, qb, js: (0, hp)),
                  pl.BlockSpec((1, 2, nb, 1, ATT_BLOCK), lambda hp, qb, js: (hp, 0, 0, 0, 0))],
        out_specs=pl.BlockSpec((ATT_BLOCK, LANES), lambda hp, qb, js: (qb, hp)),
        scratch_shapes=[pltpu.VMEM((2, ATT_BLOCK, LANES), F32), pltpu.VMEM((2, ATT_BLOCK, LANES), F32),
                        pltpu.VMEM((2, ATT_BLOCK, LANES), F32)])
    return pl.pallas_call(
        _attn_prompt_kernel, grid_spec=grid_spec,
        out_shape=jax.ShapeDtypeStruct((t, D_MODEL), BF16),
        compiler_params=_cparams(("arbitrary", "arbitrary"), 48),
        name="attn_prompt",
    )(jstart, q, k, v, cum4)


def _attn_sample_kernel(pt_ref, q_ref, kn_ref, vn_ref, lfn_ref, eye_ref, rep_ref, sel_ref, tri_ref,
                        kc_hbm, vc_hbm, lfc_hbm, o_ref,
                        kbuf, lfbuf, vbuf, ksem, lfsem, vsem, m_sc, l_sc, acc_sc, suf_sc):
    b = pl.program_id(0)
    n_pages = pt_ref.shape[1]
    page = kbuf.shape[1]
    q3 = q_ref[0]
    eye = eye_ref[...]

    def to_col(row):
        return jnp.sum(jnp.broadcast_to(row, (N_HEADS, N_HEADS)) * eye, axis=1, keepdims=True)

    def bcast(col):
        return jnp.broadcast_to(col, (N_HEADS, LANES))

    lfn_row = lfn_ref[0]
    lfn = to_col(lfn_row)
    q_l1 = jnp.sum(jnp.abs(q3), axis=1, keepdims=True)
    m_sc[...] = bcast(jnp.sum(q3 * kn_ref[0], axis=1, keepdims=True))
    l_sc[...] = jnp.ones_like(l_sc)
    acc_sc[...] = vn_ref[0]
    suf_sc[...] = jnp.zeros_like(suf_sc)

    def k_copy(pg, slot):
        return pltpu.make_async_copy(kc_hbm.at[pt_ref[b, pg]], kbuf.at[slot], ksem.at[slot])

    def lf_copy(pg, slot):
        return pltpu.make_async_copy(lfc_hbm.at[pt_ref[b, pg]], lfbuf.at[slot], lfsem.at[slot])

    k_copy(n_pages - 1, 0).start()
    lf_copy(n_pages - 1, 0).start()

    def body(i, carry):
        pg = n_pages - 1 - i
        slot = lax.rem(i, 2)

        @pl.when(i + 1 < n_pages)
        def _():
            k_copy(pg - 1, 1 - slot).start()
            lf_copy(pg - 1, 1 - slot).start()

        k_copy(pg, slot).wait()
        lf_copy(pg, slot).wait()
        k3 = kbuf[slot]
        lf = lfbuf[slot]
        page_sum = jnp.sum(lf, axis=0, keepdims=True)
        k_abs = jnp.max(jnp.max(jnp.abs(k3), axis=0), axis=1, keepdims=True)
        suf = suf_sc[:, 0:1]
        m_old = m_sc[:, 0:1]
        needed = jnp.max(k_abs * q_l1 + lfn + suf - m_old) >= -SKIP_LOGIT_GAP

        @pl.when(needed)
        def _():
            v_copy = pltpu.make_async_copy(vc_hbm.at[pt_ref[b, pg]], vbuf, vsem)
            v_copy.start()
            incl = jnp.dot(tri_ref[...], lf, precision=HI, preferred_element_type=F32)
            bias2 = lfn_row + page_sum - incl
            hi_ = bias2.astype(BF16)
            mid = (bias2 - hi_.astype(F32)).astype(BF16)
            lo = (bias2 - hi_.astype(F32) - mid.astype(F32)).astype(BF16)
            parts = jnp.concatenate([hi_, mid, lo], axis=1)
            t = jnp.dot(rep_ref[...], parts, preferred_element_type=F32)
            bias_col = jnp.sum(t * sel_ref[...], axis=1, keepdims=True)
            bias3 = bias_col.reshape(page, N_HEADS, 1) + suf[None]
            sc = jnp.sum(k3 * q3[None], axis=2, keepdims=True) + bias3
            m_new = jnp.maximum(m_old, jnp.max(sc, axis=0))
            alpha = jnp.exp(m_old - m_new)
            p3 = jnp.exp(sc - m_new[None])
            v_copy.wait()
            l_sc[...] = bcast(alpha * l_sc[:, 0:1] + jnp.sum(p3, axis=0))
            acc_sc[...] = alpha * acc_sc[...] + jnp.sum(p3 * vbuf[...], axis=0)
            m_sc[...] = bcast(m_new)

        suf_sc[...] = bcast(suf + to_col(page_sum))
        return carry

    lax.fori_loop(0, n_pages, body, 0)
    o_ref[0] = acc_sc[...] / l_sc[:, 0:1]


def _attn_sample(q, cache_k, cache_v, cache_logf, page_table, k_new, v_new, lf_new):
    nseq, n_pages = page_table.shape
    page = cache_k.shape[1]
    tri = np.tril(np.ones((page, page), np.float32))
    rep = np.repeat(np.eye(page, dtype=np.float32), N_HEADS, axis=0)
    sel = np.tile(np.tile(np.eye(N_HEADS, dtype=np.float32), (1, 3)), (page, 1))
    heads = lambda a: a.reshape(nseq, N_HEADS, HEAD_DIM)
    per_seq = lambda s, w: pl.BlockSpec((1, s, w), lambda b, pt: (b, 0, 0))
    const = lambda shape: pl.BlockSpec(shape, lambda b, pt: (0, 0))
    hbm = pl.BlockSpec(memory_space=pl.ANY)
    grid_spec = pltpu.PrefetchScalarGridSpec(
        num_scalar_prefetch=1,
        grid=(nseq,),
        in_specs=[per_seq(N_HEADS, HEAD_DIM), per_seq(N_HEADS, HEAD_DIM), per_seq(N_HEADS, HEAD_DIM),
                  per_seq(1, N_HEADS), const((N_HEADS, N_HEADS)), const(rep.shape), const(sel.shape),
                  const((page, page)), hbm, hbm, hbm],
        out_specs=per_seq(N_HEADS, HEAD_DIM),
        scratch_shapes=[pltpu.VMEM((2, page, N_HEADS, HEAD_DIM), F32), pltpu.VMEM((2, page, N_HEADS), F32),
                        pltpu.VMEM((page, N_HEADS, HEAD_DIM), F32),
                        pltpu.SemaphoreType.DMA((2,)), pltpu.SemaphoreType.DMA((2,)), pltpu.SemaphoreType.DMA(()),
                        pltpu.VMEM((N_HEADS, LANES), F32), pltpu.VMEM((N_HEADS, LANES), F32),
                        pltpu.VMEM((N_HEADS, HEAD_DIM), F32), pltpu.VMEM((N_HEADS, LANES), F32)])
    out = pl.pallas_call(
        _attn_sample_kernel, grid_spec=grid_spec,
        out_shape=jax.ShapeDtypeStruct((nseq, N_HEADS, HEAD_DIM), F32),
        compiler_params=_cparams(("arbitrary",), 32),
        name="attn_sample",
    )(page_table, heads(q), heads(k_new), heads(v_new), lf_new.reshape(nseq, 1, N_HEADS),
      jnp.eye(N_HEADS, dtype=F32), jnp.asarray(rep, BF16), jnp.asarray(sel), jnp.asarray(tri),
      cache_k, cache_v, cache_logf)
    return out.reshape(nseq, D_MODEL)


def _wo_kernel(x_ref, a_ref, w_ref, o_ref, *, hi):
    o_ref[...] = x_ref[...] + _dot(a_ref[...], w_ref[...], hi)


def _wo(x, attn, w, *, tile, hi):
    n = x.shape[0]
    row = pl.BlockSpec((tile, D_MODEL), lambda i: (i, 0))
    return pl.pallas_call(
        functools.partial(_wo_kernel, hi=hi),
        grid=(n // tile,),
        in_specs=[row, row, pl.BlockSpec((D_MODEL, D_MODEL), lambda i: (0, 0))],
        out_specs=row,
        out_shape=jax.ShapeDtypeStruct((n, D_MODEL), F32),
        compiler_params=_cparams(("arbitrary",), 32),
        name="wo_hi" if hi else "wo",
    )(x, attn, w if hi else w.astype(BF16))


MOE_TILE = 1024
PROJ_TILE = 256
WO_TILE = 512


def kernel(x_prompt, x_sample, state_ssm_re, state_ssm_im, cache_k, cache_v, cache_logf, page_table, norm_mix, norm_ffn, ssm_lambda_re, ssm_lambda_im, ssm_log_dt, ssm_b_re, ssm_b_im, ssm_c_re, ssm_c_im, ssm_d, ssm_w_glu, ssm_b_glu, kv_norm, w_kvf, b_forget, k_norm, w_q, q_norm, w_o, w_router_group, w_router_expert, w_gate, w_up, w_down):
    t = x_prompt.shape[1]
    nseq = x_sample.shape[0]
    hd = N_HEADS * HEAD_DIM
    xp = x_prompt.reshape(t, D_MODEL)
    xs = x_sample.reshape(nseq, D_MODEL)

    apow_re, apow_im, bb_re, bb_im = _ssm_prep(ssm_lambda_re[0], ssm_lambda_im[0], ssm_log_dt[0],
                                               ssm_b_re[0], ssm_b_im[0])
    bmat, cmat, consts, a_rows = _ssm_layouts(apow_re, apow_im, bb_re, bb_im, ssm_c_re[0], ssm_c_im[0])
    wk, wv, wf = w_kvf[:, :hd], w_kvf[:, hd:2 * hd], w_kvf[:, 2 * hd:]

    def moe(x, layer, tile, hi):
        return _moe(x, norm_ffn[layer], w_router_group[layer], w_router_expert[layer],
                    w_gate, w_up, w_down, layer=layer, tile=tile, hi=hi)

    x1, s_end = _s5_prompt(xp, norm_mix[0], bmat, cmat, consts, ssm_d[0], ssm_w_glu[0], ssm_b_glu[0])
    x2 = moe(x1, 0, MOE_TILE, False)
    k_p, v_p, lf_p, q_p, kb_p, vb_p, cum_t = _proj(x2, kv_norm, norm_mix[1], wk, wv, wf, w_q[0], k_norm, q_norm[0],
                                                  b_forget, tile=PROJ_TILE, hi=False, with_cum=True)
    jstart = _first_key_block(cum_t, q_norm[0], k_norm)
    attn_p = _attn_prompt(q_p, kb_p, vb_p, cum_t, jstart)
    x3 = _wo(x2, attn_p, w_o[0], tile=WO_TILE, hi=False)
    y_p = moe(x3, 1, MOE_TILE, False)
    sre_p, sim_p = _state_from_layout(s_end)

    s0 = _state_to_layout(state_ssm_re[0], state_ssm_im[0])
    x1s, s_new = _s5_sample(xs, norm_mix[0], bmat, cmat, a_rows, s0, ssm_d[0], ssm_w_glu[0], ssm_b_glu[0])
    x2s = moe(x1s, 0, nseq, True)
    k_s, v_s, lf_s, q_s = _proj(x2s, kv_norm, norm_mix[1], wk, wv, wf, w_q[0], k_norm, q_norm[0], b_forget,
                                tile=nseq, hi=True, with_cum=False)
    attn_s = _attn_sample(q_s, cache_k, cache_v, cache_logf, page_table, k_s, v_s, lf_s)
    x3s = _wo(x2s, attn_s, w_o[0], tile=nseq, hi=True)
    y_s = moe(x3s, 1, nseq, True)
    sre_s, sim_s = _state_from_layout(s_new)

    return (y_p.reshape(1, t, D_MODEL), y_s.reshape(nseq, 1, D_MODEL),
            sre_p[None], sim_p[None],
            k_p.reshape(1, t, N_HEADS, HEAD_DIM), v_p.reshape(1, t, N_HEADS, HEAD_DIM), lf_p.reshape(1, t, N_HEADS),
            sre_s[None], sim_s[None],
            k_s.reshape(nseq, 1, N_HEADS, HEAD_DIM), v_s.reshape(nseq, 1, N_HEADS, HEAD_DIM),
            lf_s.reshape(nseq, 1, N_HEADS))
```

```python
import functools

import numpy as np
import jax
import jax.numpy as jnp
from jax import lax
from jax.experimental import pallas as pl
from jax.experimental.pallas import tpu as pltpu

F32 = jnp.float32
BF16 = jnp.bfloat16
HI = lax.Precision.HIGHEST

D_MODEL = 1024
SSM_GROUP = 16
SSM_GROUPS = 64
SSM_STATE = 64
N_HEADS = 16
HEAD_DIM = 64
N_EXPERT_GROUPS = 4
EXPERTS_PER_GROUP = 8
N_EXPERTS = 32
D_EXPERT = 256
EPS = 1e-6
NEG = -1e30

V7X_VMEM_BYTES = 64 * 1024 * 1024
LANES = 128
SUBLANES = 8

SEG_COUNT = SUBLANES
SEG_LEN = 32
S5_TILE = SEG_COUNT * SEG_LEN
S5_BLOCKS = 4
S5_BLOCK_CH = D_MODEL // S5_BLOCKS
S5_BLOCK_ST = 2 * SSM_GROUPS * SSM_STATE // S5_BLOCKS
S5_CHUNK = 512
S5_NCHUNK = 2 * SSM_GROUPS * SSM_STATE // S5_CHUNK
S5_NPAIR = S5_NCHUNK // 2
S5_EXPS = (1, 0) + tuple(SEG_LEN * q for q in range(1, SEG_COUNT))

ATT_BLOCK = 256
SKIP_LOGIT_GAP = 100.0


def _cparams(semantics, vmem_mb):
    return pltpu.CompilerParams(dimension_semantics=semantics,
                                vmem_limit_bytes=min(vmem_mb * 1024 * 1024, V7X_VMEM_BYTES - 4 * 1024 * 1024))


def _dot(a, b, hi):
    if hi:
        return jnp.dot(a, b, precision=HI, preferred_element_type=F32)
    return jnp.dot(a.astype(BF16), b.astype(BF16), preferred_element_type=F32)


def _sigmoid(x):
    return 1.0 / (1.0 + jnp.exp(-x))


def _log_sigmoid(x):
    return jnp.minimum(x, 0.0) - jnp.log(1.0 + jnp.exp(-jnp.abs(x)))


def _gelu_tanh(x):
    c = np.float32(np.sqrt(2.0 / np.pi))
    return 0.5 * x * (1.0 + jnp.tanh(c * (x + 0.044715 * (x * x * x))))


def _rmsnorm_rows(x, gain):
    ms = jnp.mean(x * x, axis=-1, keepdims=True)
    return x * lax.rsqrt(ms + EPS) * gain


def _ssm_prep_kernel(lam_re_ref, lam_im_ref, log_dt_ref, lam_re_x_ref, lam_im_x_ref, b_re_ref, b_im_ref,
                     apow_re_ref, apow_im_ref, bb_re_ref, bb_im_ref):
    dt = jnp.exp(log_dt_ref[...])

    def disc(lr, li):
        mag = jnp.exp(lr * dt)
        ang = li * dt
        return mag * jnp.cos(ang), mag * jnp.sin(ang)

    def cmul(a, b):
        return a[0] * b[0] - a[1] * b[1], a[0] * b[1] + a[1] * b[0]

    a1 = disc(lam_re_ref[...], lam_im_ref[...])
    a_seg = a1
    for _ in range(int(np.log2(SEG_LEN))):
        a_seg = cmul(a_seg, a_seg)
    powers = {1: a1, 0: (jnp.ones_like(a1[0]), jnp.zeros_like(a1[0])), SEG_LEN: a_seg}
    for q in range(2, SEG_COUNT):
        powers[SEG_LEN * q] = cmul(powers[SEG_LEN * (q - 1)], a_seg)
    for i, e in enumerate(S5_EXPS):
        apow_re_ref[i] = powers[e][0]
        apow_im_ref[i] = powers[e][1]

    lrx, lix = lam_re_x_ref[...], lam_im_x_ref[...]
    arx, aix = disc(lrx, lix)
    den = lrx * lrx + lix * lix
    nr, ni = arx - 1.0, aix
    wr = (nr * lrx + ni * lix) / den
    wi = (ni * lrx - nr * lix) / den
    br, bi = b_re_ref[...], b_im_ref[...]
    bb_re_ref[...] = wr * br - wi * bi
    bb_im_ref[...] = wr * bi + wi * br


def _ssm_prep(lam_re, lam_im, log_dt, b_re, b_im):
    g, p, h = SSM_GROUPS, SSM_STATE, SSM_GROUP
    rep = lambda a: jnp.repeat(a, h, axis=1)
    n_exp = len(S5_EXPS)
    out_shape = (jax.ShapeDtypeStruct((n_exp, g, p), F32), jax.ShapeDtypeStruct((n_exp, g, p), F32),
                 jax.ShapeDtypeStruct((g, p * h), F32), jax.ShapeDtypeStruct((g, p * h), F32))
    apow_re, apow_im, bb_re, bb_im = pl.pallas_call(
        _ssm_prep_kernel, out_shape=out_shape, name="ssm_prep",
    )(lam_re, lam_im, log_dt.reshape(g, 1), rep(lam_re), rep(lam_im),
      b_re.reshape(g, p * h), b_im.reshape(g, p * h))
    return apow_re, apow_im, bb_re.reshape(g, p, h), bb_im.reshape(g, p, h)


def _ssm_layouts(apow_re, apow_im, bb_re, bb_im, c_re, c_im):
    nb, gl = S5_BLOCKS, SSM_GROUPS // S5_BLOCKS
    eye = jnp.eye(gl, dtype=F32)

    def embed_b(bb):
        v = bb.transpose(0, 2, 1).reshape(nb, gl, SSM_GROUP, SSM_STATE)
        return jnp.einsum('jghp,gk->jghkp', v, eye).reshape(nb, gl * SSM_GROUP, gl * SSM_STATE)

    def embed_c(c):
        v = c.transpose(0, 2, 1).reshape(nb, gl, SSM_STATE, SSM_GROUP)
        return jnp.einsum('jgph,gk->jgpkh', v, eye).reshape(nb, gl * SSM_STATE, gl * SSM_GROUP)

    bmat = jnp.concatenate([embed_b(bb_re), embed_b(bb_im)], axis=2)
    cmat = jnp.concatenate([embed_c(c_re), -embed_c(c_im)], axis=1)

    flat = lambda a: a.reshape(a.shape[0], S5_NPAIR, S5_CHUNK)
    pr, pi = flat(apow_re), flat(apow_im)
    idx = {e: i for i, e in enumerate(S5_EXPS)}
    rows = [pr[idx[1]], pi[idx[1]]]
    for d in (1, 2, 4):
        rows += [pr[idx[SEG_LEN * d]], pi[idx[SEG_LEN * d]]]
    rows += [pr[idx[SEG_LEN * q]] for q in range(SEG_COUNT)]
    rows += [pi[idx[SEG_LEN * q]] for q in range(SEG_COUNT)]
    consts = jnp.stack(rows, axis=1)
    a_rows = jnp.stack([apow_re[idx[1]].reshape(-1), apow_im[idx[1]].reshape(-1)])
    return bmat, cmat, consts, a_rows


def _state_to_layout(s_re, s_im):
    b = s_re.shape[0]
    r = s_re.reshape(b, S5_BLOCKS, 1, S5_BLOCK_ST // 2)
    i = s_im.reshape(b, S5_BLOCKS, 1, S5_BLOCK_ST // 2)
    return jnp.concatenate([r, i], axis=2).reshape(b, S5_BLOCKS * S5_BLOCK_ST)


def _state_from_layout(s):
    b = s.shape[0]
    v = s.reshape(b, S5_BLOCKS, 2, S5_BLOCK_ST // 2)
    return (v[:, :, 0].reshape(b, SSM_GROUPS, SSM_STATE), v[:, :, 1].reshape(b, SSM_GROUPS, SSM_STATE))


def _segment_permutation():
    m = np.zeros((S5_TILE, S5_TILE), np.float32)
    for rp in range(S5_TILE):
        k, q = divmod(rp, SEG_COUNT)
        m[rp, q * SEG_LEN + k] = 1.0
    return m


def _s5_prompt_kernel(x_ref, gain_ref, perm_ref, perm_t_ref, bmat_ref, cmat_ref, const_ref, dskip_ref,
                      wglu_ref, bglu_ref, o_ref, state_ref, pbuf, sbuf, carry):
    i = pl.program_id(0)

    @pl.when(i == 0)
    def _():
        carry[...] = jnp.zeros_like(carry)

    x = x_ref[...]
    u = _rmsnorm_rows(x, gain_ref[...])
    up = jnp.dot(perm_ref[...], u.astype(BF16), preferred_element_type=F32).astype(BF16)
    for j in range(S5_BLOCKS):
        pj = jnp.dot(up[:, j * S5_BLOCK_CH:(j + 1) * S5_BLOCK_CH], bmat_ref[j], preferred_element_type=F32)
        for c in range(S5_BLOCK_ST // S5_CHUNK):
            pbuf[j * (S5_BLOCK_ST // S5_CHUNK) + c] = pj[:, c * S5_CHUNK:(c + 1) * S5_CHUNK]

    row = lax.broadcasted_iota(jnp.int32, (SEG_COUNT, S5_CHUNK), 0)

    def chunk_pair(cp, _):
        c_re = (cp // 2) * 4 + (cp % 2)
        c_im = c_re + 2
        cst = const_ref[cp]
        ar = jnp.broadcast_to(cst[0:1], (SEG_COUNT, S5_CHUNK))
        ai = jnp.broadcast_to(cst[1:2], (SEG_COUNT, S5_CHUNK))

        def step(k, sr, si):
            pr = pbuf[c_re, k * SEG_COUNT:(k + 1) * SEG_COUNT, :]
            pi = pbuf[c_im, k * SEG_COUNT:(k + 1) * SEG_COUNT, :]
            return ar * sr - ai * si + pr, ar * si + ai * sr + pi

        er = jnp.zeros((SEG_COUNT, S5_CHUNK), F32)
        ei = jnp.zeros((SEG_COUNT, S5_CHUNK), F32)
        for k in range(SEG_LEN):
            er, ei = step(k, er, ei)

        for n, d in enumerate((1, 2, 4)):
            cr, ci = cst[2 + 2 * n:3 + 2 * n], cst[3 + 2 * n:4 + 2 * n]
            shr = jnp.where(row >= d, pltpu.roll(er, d, 0), 0.0)
            shi = jnp.where(row >= d, pltpu.roll(ei, d, 0), 0.0)
            er, ei = er + cr * shr - ci * shi, ei + cr * shi + ci * shr
        exr = jnp.where(row >= 1, pltpu.roll(er, 1, 0), 0.0)
        exi = jnp.where(row >= 1, pltpu.roll(ei, 1, 0), 0.0)
        c0r = carry[c_re, SEG_COUNT - 1:SEG_COUNT, :]
        c0i = carry[c_im, SEG_COUNT - 1:SEG_COUNT, :]
        qr, qi = cst[8:16], cst[16:24]
        sr = exr + qr * c0r - qi * c0i
        si = exi + qr * c0i + qi * c0r

        for k2 in range(SEG_LEN // 2):
            sr1, si1 = step(2 * k2, sr, si)
            sr, si = step(2 * k2 + 1, sr1, si1)
            rows = slice(2 * k2 * SEG_COUNT, (2 * k2 + 2) * SEG_COUNT)
            sbuf[c_re, rows, :] = jnp.concatenate([sr1, sr], axis=0).astype(BF16)
            sbuf[c_im, rows, :] = jnp.concatenate([si1, si], axis=0).astype(BF16)
        carry[c_re] = sr
        carry[c_im] = si
        return 0

    lax.fori_loop(0, S5_NPAIR, chunk_pair, 0)

    ys = []
    for j in range(S5_BLOCKS):
        acc = None
        for c in range(S5_BLOCK_ST // S5_CHUNK):
            t = jnp.dot(sbuf[j * (S5_BLOCK_ST // S5_CHUNK) + c], cmat_ref[j, c * S5_CHUNK:(c + 1) * S5_CHUNK, :],
                        preferred_element_type=F32)
            acc = t if acc is None else acc + t
        ys.append(acc)
    yp = jnp.concatenate(ys, axis=1)
    yh = yp.astype(BF16)
    yl = (yp - yh.astype(F32)).astype(BF16)
    pt = perm_t_ref[...]
    y = (jnp.dot(pt, yh, preferred_element_type=F32) + jnp.dot(pt, yl, preferred_element_type=F32)
         + dskip_ref[...] * u)
    z = _gelu_tanh(y)
    zz = jnp.dot(z.astype(BF16), wglu_ref[...], preferred_element_type=F32) + bglu_ref[...]
    o_ref[...] = x + zz[:, :D_MODEL] * _sigmoid(zz[:, D_MODEL:])

    @pl.when(i == pl.num_programs(0) - 1)
    def _():
        state_ref[...] = carry[...]


def _s5_prompt(x, gain, bmat, cmat, consts, dskip, wglu, bglu):
    t = x.shape[0]
    assert t % S5_TILE == 0
    perm = _segment_permutation()
    const_spec = lambda shape: pl.BlockSpec(shape, lambda i: (0,) * len(shape))
    y, state = pl.pallas_call(
        _s5_prompt_kernel,
        grid=(t // S5_TILE,),
        in_specs=[pl.BlockSpec((S5_TILE, D_MODEL), lambda i: (i, 0)),
                  const_spec((1, D_MODEL)),
                  const_spec((S5_TILE, S5_TILE)), const_spec((S5_TILE, S5_TILE)),
                  const_spec(bmat.shape), const_spec(cmat.shape), const_spec(consts.shape),
                  const_spec((1, D_MODEL)),
                  const_spec((D_MODEL, 2 * D_MODEL)), const_spec((1, 2 * D_MODEL))],
        out_specs=[pl.BlockSpec((S5_TILE, D_MODEL), lambda i: (i, 0)),
                   const_spec((S5_NCHUNK, SEG_COUNT, S5_CHUNK))],
        out_shape=[jax.ShapeDtypeStruct((t, D_MODEL), F32),
                   jax.ShapeDtypeStruct((S5_NCHUNK, SEG_COUNT, S5_CHUNK), F32)],
        scratch_shapes=[pltpu.VMEM((S5_NCHUNK, S5_TILE, S5_CHUNK), F32),
                        pltpu.VMEM((S5_NCHUNK, S5_TILE, S5_CHUNK), BF16),
                        pltpu.VMEM((S5_NCHUNK, SEG_COUNT, S5_CHUNK), F32)],
        compiler_params=_cparams(("arbitrary",), 56),
        name="s5_prompt",
    )(x, gain.reshape(1, -1), jnp.asarray(perm, BF16), jnp.asarray(perm.T, BF16),
      bmat.astype(BF16), cmat.astype(BF16), consts, dskip.reshape(1, -1), wglu.astype(BF16), bglu.reshape(1, -1))
    s = state[:, SEG_COUNT - 1, :].reshape(1, S5_BLOCKS * S5_BLOCK_ST)
    return y, s


def _s5_sample_kernel(x_ref, gain_ref, bmat_ref, cmat_ref, a_ref, s0_ref, dskip_ref, wglu_ref, bglu_ref,
                      o_ref, s_ref):
    x = x_ref[...]
    u = _rmsnorm_rows(x, gain_ref[...])
    half = S5_BLOCK_ST // 2
    ys = []
    for j in range(S5_BLOCKS):
        pj = jnp.dot(u[:, j * S5_BLOCK_CH:(j + 1) * S5_BLOCK_CH], bmat_ref[j], precision=HI,
                     preferred_element_type=F32)
        ar = a_ref[0:1, j * half:(j + 1) * half]
        ai = a_ref[1:2, j * half:(j + 1) * half]
        s0r = s0_ref[:, j * S5_BLOCK_ST:j * S5_BLOCK_ST + half]
        s0i = s0_ref[:, j * S5_BLOCK_ST + half:(j + 1) * S5_BLOCK_ST]
        sr = ar * s0r - ai * s0i + pj[:, :half]
        si = ar * s0i + ai * s0r + pj[:, half:]
        s_ref[:, j * S5_BLOCK_ST:j * S5_BLOCK_ST + half] = sr
        s_ref[:, j * S5_BLOCK_ST + half:(j + 1) * S5_BLOCK_ST] = si
        ys.append(jnp.dot(sr, cmat_ref[j, :half, :], precision=HI, preferred_element_type=F32)
                  + jnp.dot(si, cmat_ref[j, half:, :], precision=HI, preferred_element_type=F32))
    y = jnp.concatenate(ys, axis=1) + dskip_ref[...] * u
    z = _gelu_tanh(y)
    zz = jnp.dot(z, wglu_ref[...], precision=HI, preferred_element_type=F32) + bglu_ref[...]
    o_ref[...] = x + zz[:, :D_MODEL] * _sigmoid(zz[:, D_MODEL:])


def _s5_sample(x, gain, bmat, cmat, a_rows, s0, dskip, wglu, bglu):
    n = x.shape[0]
    return pl.pallas_call(
        _s5_sample_kernel,
        out_shape=[jax.ShapeDtypeStruct((n, D_MODEL), F32), jax.ShapeDtypeStruct(s0.shape, F32)],
        compiler_params=_cparams(None, 56),
        name="s5_sample",
    )(x, gain.reshape(1, -1), bmat, cmat, a_rows, s0, dskip.reshape(1, -1), wglu, bglu.reshape(1, -1))


def _route(xn, wrg, wre):
    gl = jnp.dot(xn, wrg, precision=HI, preferred_element_type=F32)
    el = jnp.dot(xn, wre, precision=HI, preferred_element_type=F32)
    rows = xn.shape[0]
    lane_g = lax.broadcasted_iota(jnp.int32, (rows, N_EXPERT_GROUPS), 1)
    gmax = jnp.max(gl, axis=1, keepdims=True)
    gidx = jnp.min(jnp.where(gl == gmax, lane_g, N_EXPERT_GROUPS), axis=1, keepdims=True)
    gw = 1.0 / jnp.sum(jnp.exp(gl - gmax), axis=1, keepdims=True)
    lane_e = lax.broadcasted_iota(jnp.int32, (rows, N_EXPERTS), 1)
    in_group = lax.shift_right_logical(lane_e, 3) == gidx
    m1 = jnp.where(in_group, el, -jnp.inf)
    v1 = jnp.max(m1, axis=1, keepdims=True)
    i1 = jnp.min(jnp.where(m1 == v1, lane_e, N_EXPERTS), axis=1, keepdims=True)
    m2 = jnp.where(lane_e == i1, -jnp.inf, m1)
    v2 = jnp.max(m2, axis=1, keepdims=True)
    i2 = jnp.min(jnp.where(m2 == v2, lane_e, N_EXPERTS), axis=1, keepdims=True)
    e2 = jnp.exp(v2 - v1)
    p1 = 1.0 / (1.0 + e2)
    p2 = e2 / (1.0 + e2)
    return jnp.where(lane_e == i1, gw * p1, 0.0) + jnp.where(lane_e == i2, gw * p2, 0.0)


def _moe_kernel(x_ref, gain_ref, wrg_ref, wre_ref, wg_ref, wu_ref, wd_ref, o_ref, xn_sc, comb_sc, acc_sc, *, hi):
    e = pl.program_id(1)

    @pl.when(e == 0)
    def _():
        xn = _rmsnorm_rows(x_ref[...], gain_ref[...])
        comb_sc[...] = _route(xn, wrg_ref[...], wre_ref[...])
        xn_sc[...] = xn.astype(xn_sc.dtype)
        acc_sc[...] = jnp.zeros_like(acc_sc)

    xn = xn_sc[...]
    g = _dot(xn, wg_ref[0], hi)
    u = _dot(xn, wu_ref[0], hi)
    lane_e = lax.broadcasted_iota(jnp.int32, comb_sc.shape, 1)
    cw = jnp.sum(jnp.where(lane_e == e, comb_sc[...], 0.0), axis=1, keepdims=True)
    h = (g * _sigmoid(g)) * u * cw
    acc_sc[...] += _dot(h, wd_ref[0], hi)

    @pl.when(e == pl.num_programs(1) - 1)
    def _():
        o_ref[...] = x_ref[...] + acc_sc[...]


def _moe(x, gain, wrg, wre, wg, wu, wd, *, layer, tile, hi):
    n = x.shape[0]
    assert n % tile == 0
    row_spec = pl.BlockSpec((tile, D_MODEL), lambda i, e: (i, 0))
    const2 = lambda shape: pl.BlockSpec(shape, lambda i, e: (0, 0))
    return pl.pallas_call(
        functools.partial(_moe_kernel, hi=hi),
        grid=(n // tile, N_EXPERTS),
        in_specs=[row_spec, const2((1, D_MODEL)), const2(wrg.shape), const2(wre.shape),
                  pl.BlockSpec((None, 1, D_MODEL, D_EXPERT), lambda i, e: (layer, e, 0, 0)),
                  pl.BlockSpec((None, 1, D_MODEL, D_EXPERT), lambda i, e: (layer, e, 0, 0)),
                  pl.BlockSpec((None, 1, D_EXPERT, D_MODEL), lambda i, e: (layer, e, 0, 0))],
        out_specs=row_spec,
        out_shape=jax.ShapeDtypeStruct((n, D_MODEL), F32),
        scratch_shapes=[pltpu.VMEM((tile, D_MODEL), F32 if hi else BF16),
                        pltpu.VMEM((tile, N_EXPERTS), F32),
                        pltpu.VMEM((tile, D_MODEL), F32)],
        compiler_params=_cparams(("arbitrary", "arbitrary"), 48),
        name="moe_hi" if hi else "moe",
    )(x, gain.reshape(1, -1), wrg, wre, wg, wu, wd)


def _head_ones():
    m = np.zeros((D_MODEL, D_MODEL), np.float32)
    for h in range(N_HEADS):
        m[h * HEAD_DIM:(h + 1) * HEAD_DIM, h * HEAD_DIM:(h + 1) * HEAD_DIM] = 1.0
    return m


def _head_rmsnorm(v, hh, gain, hi):
    ssq = _dot(v * v, hh, hi)
    return v * lax.rsqrt(ssq * (1.0 / HEAD_DIM) + EPS) * gain


def _proj_kernel(x_ref, kvn_ref, qn_ref, wk_ref, wv_ref, wf_ref, wft_ref, wq_ref, hh_ref, kg_ref, qg_ref,
                 bf_ref, bfc_ref, tri_ref, *out_and_scratch, hi, with_cum):
    if with_cum:
        k_ref, v_ref, lf_ref, qs_ref, kb_ref, vb_ref, cum_ref, carry = out_and_scratch
    else:
        k_ref, v_ref, lf_ref, qs_ref = out_and_scratch
    x = x_ref[...]
    ms = jnp.mean(x * x, axis=-1, keepdims=True)
    xhat = x * lax.rsqrt(ms + EPS)
    xn1 = xhat * kvn_ref[...]
    xn2 = xhat * qn_ref[...]
    hh = hh_ref[...]
    k = _head_rmsnorm(_dot(xn1, wk_ref[...], hi), hh, kg_ref[...], hi)
    v = _dot(xn1, wv_ref[...], hi)
    q = _head_rmsnorm(_dot(xn2, wq_ref[...], hi), hh, qg_ref[...], hi) * (HEAD_DIM ** -0.5)
    f = jnp.dot(xn1, wf_ref[...], precision=HI, preferred_element_type=F32)
    k_ref[...] = k
    v_ref[...] = v
    lf_ref[...] = _log_sigmoid(f + bf_ref[...])
    qs_ref[...] = q.astype(qs_ref.dtype)
    if with_cum:
        i = pl.program_id(0)

        @pl.when(i == 0)
        def _():
            carry[...] = jnp.zeros_like(carry)

        kb_ref[...] = k.astype(BF16)
        vb_ref[...] = v.astype(BF16)
        ft = lax.dot_general(wft_ref[...], xn1, (((1,), (1,)), ((), ())), precision=HI,
                             preferred_element_type=F32)
        lft = _log_sigmoid(ft + bfc_ref[...])
        cum = jnp.dot(lft, tri_ref[...], precision=HI, preferred_element_type=F32) + carry[:, 0:1]
        cum_ref[...] = cum
        carry[...] = jnp.broadcast_to(cum[:, cum.shape[1] - 1:], carry.shape)


def _proj(x, kv_norm, q_in_norm, wk, wv, wf, wq, k_norm, q_norm, b_forget, *, tile, hi, with_cum):
    n = x.shape[0]
    assert n % tile == 0
    wdt = F32 if hi else BF16
    row = lambda w: pl.BlockSpec((tile, w), lambda i: (i, 0))
    const = lambda shape: pl.BlockSpec(shape, lambda i: (0, 0))
    tri = np.triu(np.ones((tile, tile), np.float32))
    gain_tiled = lambda g: jnp.tile(g, N_HEADS).reshape(1, D_MODEL)
    out_specs = [row(D_MODEL), row(D_MODEL), row(N_HEADS), row(D_MODEL)]
    out_shape = [jax.ShapeDtypeStruct((n, D_MODEL), F32), jax.ShapeDtypeStruct((n, D_MODEL), F32),
                 jax.ShapeDtypeStruct((n, N_HEADS), F32), jax.ShapeDtypeStruct((n, D_MODEL), F32 if hi else BF16)]
    scratch = []
    if with_cum:
        out_specs += [row(D_MODEL), row(D_MODEL), pl.BlockSpec((N_HEADS, tile), lambda i: (0, i))]
        out_shape += [jax.ShapeDtypeStruct((n, D_MODEL), BF16), jax.ShapeDtypeStruct((n, D_MODEL), BF16),
                      jax.ShapeDtypeStruct((N_HEADS, n), F32)]
        scratch = [pltpu.VMEM((N_HEADS, LANES), F32)]
    return pl.pallas_call(
        functools.partial(_proj_kernel, hi=hi, with_cum=with_cum),
        grid=(n // tile,),
        in_specs=[row(D_MODEL), const((1, D_MODEL)), const((1, D_MODEL)),
                  const((D_MODEL, D_MODEL)), const((D_MODEL, D_MODEL)), const((D_MODEL, N_HEADS)),
                  const((N_HEADS, D_MODEL)), const((D_MODEL, D_MODEL)), const((D_MODEL, D_MODEL)),
                  const((1, D_MODEL)), const((1, D_MODEL)), const((1, N_HEADS)), const((N_HEADS, 1)),
                  const((tile, tile))],
        out_specs=out_specs, out_shape=out_shape, scratch_shapes=scratch,
        compiler_params=_cparams(("arbitrary",), 56),
        name="proj_hi" if hi else "proj",
    )(x, kv_norm.reshape(1, -1), q_in_norm.reshape(1, -1), wk.astype(wdt), wv.astype(wdt), wf, wf.T,
      wq.astype(wdt), jnp.asarray(_head_ones(), wdt), gain_tiled(k_norm), gain_tiled(q_norm),
      b_forget.reshape(1, -1), b_forget.reshape(-1, 1), jnp.asarray(tri))


def _attn_prompt_kernel(jstart_ref, q_ref, k_ref, v_ref, cum_ref, o_ref, m_sc, l_sc, acc_sc):
    hp = pl.program_id(0)
    qb = pl.program_id(1)
    blk = ATT_BLOCK
    q2 = q_ref[...]
    lane = lax.broadcasted_iota(jnp.int32, (blk, LANES), 1)
    first = lane < HEAD_DIM
    qh = (jnp.where(first, q2, jnp.zeros_like(q2)), jnp.where(first, jnp.zeros_like(q2), q2))
    m_sc[...] = jnp.full(m_sc.shape, NEG, F32)
    l_sc[...] = jnp.zeros_like(l_sc)
    acc_sc[...] = jnp.zeros_like(acc_sc)
    base = [cum_ref[0, h, qb, :, 0:1] for h in range(2)]
    row_i = lax.broadcasted_iota(jnp.int32, (blk, blk), 0)
    col_i = lax.broadcasted_iota(jnp.int32, (blk, blk), 1)

    def visit(j, diagonal):
        start = pl.multiple_of(j * blk, blk)
        kblk = k_ref[pl.ds(start, blk), :]
        vblk = v_ref[pl.ds(start, blk), :]
        for h in range(2):
            s = lax.dot_general(qh[h], kblk, (((1,), (1,)), ((), ())), preferred_element_type=F32)
            s = s - (cum_ref[0, h, j] - base[h])
            if diagonal:
                s = jnp.where(col_i <= row_i, s, NEG)
            m_old = m_sc[h]
            m_new = jnp.maximum(m_old, jnp.max(s, axis=1, keepdims=True))
            alpha = jnp.exp(m_old - m_new)
            p = jnp.exp(s - m_new[:, 0:1])
            l_sc[h] = alpha * l_sc[h] + jnp.sum(p, axis=1, keepdims=True)
            acc_sc[h] = alpha * acc_sc[h] + jnp.dot(p.astype(BF16), vblk, preferred_element_type=F32)
            m_sc[h] = m_new

    def body(j, carry):
        visit(j, False)
        return carry

    lax.fori_loop(jstart_ref[hp, qb], qb, body, 0)
    visit(qb, True)
    o = jnp.where(first, acc_sc[0] / l_sc[0], acc_sc[1] / l_sc[1])
    o_ref[...] = o.astype(o_ref.dtype)


def _first_key_block(cum_t, q_norm, k_norm):
    t = cum_t.shape[1]
    nb = t // ATT_BLOCK
    bound = 1.02 * HEAD_DIM ** 0.5 * jnp.max(jnp.abs(q_norm)) * jnp.max(jnp.abs(k_norm))
    limit = SKIP_LOGIT_GAP + 2.0 * bound
    c_first_q = cum_t[:, ::ATT_BLOCK]
    c_last_k = cum_t[:, ATT_BLOCK - 1::ATT_BLOCK]
    gap = c_first_q[:, :, None] - c_last_k[:, None, :]
    j_idx = jnp.arange(nb)
    skip = (gap < -limit) & (j_idx[None, None, :] < j_idx[None, :, None])
    first = jnp.sum(skip, axis=2).astype(jnp.int32)
    return jnp.min(first.reshape(N_HEADS // 2, 2, nb), axis=1)


def _attn_prompt(q, k, v, cum_t, jstart):
    t = q.shape[0]
    nb = t // ATT_BLOCK
    cum4 = cum_t.reshape(N_HEADS // 2, 2, nb, 1, ATT_BLOCK)
    grid_spec = pltpu.PrefetchScalarGridSpec(
        num_scalar_prefetch=1,
        grid=(N_HEADS // 2, nb),
        in_specs=[pl.BlockSpec((ATT_BLOCK, LANES), lambda hp, qb, js: (qb, hp)),
                  pl.BlockSpec((t, LANES), lambda hp, qb, js: (0, hp)),
                  pl.BlockSpec((t, LANES), lambda hp, qb, js: (0, hp)),
                  pl.BlockSpec((1, 2, nb, 1, ATT_BLOCK), lambda hp, qb, js: (hp, 0, 0, 0, 0))],
        out_specs=pl.BlockSpec((ATT_BLOCK, LANES), lambda hp, qb, js: (qb, hp)),
        scratch_shapes=[pltpu.VMEM((2, ATT_BLOCK, LANES), F32), pltpu.VMEM((2, ATT_BLOCK, LANES), F32),
                        pltpu.VMEM((2, ATT_BLOCK, LANES), F32)])
    return pl.pallas_call(
        _attn_prompt_kernel, grid_spec=grid_spec,
        out_shape=jax.ShapeDtypeStruct((t, D_MODEL), BF16),
        compiler_params=_cparams(("arbitrary", "arbitrary"), 48),
        name="attn_prompt",
    )(jstart, q, k, v, cum4)


def _attn_sample_kernel(pt_ref, q_ref, qc_ref, kn_ref, vn_ref, lfn_ref, tri_ref,
                        kc_hbm, vc_hbm, lfc_hbm, o_ref,
                        kbuf, lfbuf, vbuf, ksem, lfsem, vsem, m_sc, l_sc, acc_sc, suf_sc):
    b = pl.program_id(0)
    n_pages = pt_ref.shape[1]
    q2 = q_ref[0]
    q3 = qc_ref[0]

    def bcast(col):
        return jnp.broadcast_to(col, (N_HEADS, LANES))

    lfn = lfn_ref[0]
    q_l1 = jnp.sum(jnp.abs(q2), axis=1, keepdims=True)
    m_sc[...] = bcast(jnp.sum(q2 * kn_ref[0], axis=1, keepdims=True))
    l_sc[...] = jnp.ones_like(l_sc)
    acc_sc[...] = vn_ref[0]
    suf_sc[...] = jnp.zeros_like(suf_sc)

    def k_copy(pg, slot):
        return pltpu.make_async_copy(kc_hbm.at[pt_ref[b, pg]], kbuf.at[slot], ksem.at[slot])

    def lf_copy(pg, slot):
        return pltpu.make_async_copy(lfc_hbm.at[pt_ref[b, pg]], lfbuf.at[slot], lfsem.at[slot])

    k_copy(n_pages - 1, 0).start()
    lf_copy(n_pages - 1, 0).start()

    def body(i, carry):
        pg = n_pages - 1 - i
        slot = lax.rem(i, 2)

        @pl.when(i + 1 < n_pages)
        def _():
            k_copy(pg - 1, 1 - slot).start()
            lf_copy(pg - 1, 1 - slot).start()

        k_copy(pg, slot).wait()
        lf_copy(pg, slot).wait()
        k3 = kbuf[slot]
        lf = lfbuf[slot]
        page_sum = jnp.sum(lf, axis=1, keepdims=True)
        k_abs = jnp.max(jnp.max(jnp.abs(k3), axis=1), axis=1, keepdims=True)
        suf = suf_sc[:, 0:1]
        m_old = m_sc[:, 0:1]
        needed = jnp.max(k_abs * q_l1 + lfn + suf - m_old) >= -SKIP_LOGIT_GAP

        @pl.when(needed)
        def _():
            v_copy = pltpu.make_async_copy(vc_hbm.at[pt_ref[b, pg]], vbuf, vsem)
            v_copy.start()
            incl = jnp.dot(lf, tri_ref[...], precision=HI, preferred_element_type=F32)
            sc = jnp.sum(k3 * q3, axis=1) + (lfn + suf + page_sum - incl)
            m_new = jnp.maximum(m_old, jnp.max(sc, axis=1, keepdims=True))
            alpha = jnp.exp(m_old - m_new)
            p = jnp.exp(sc - m_new)
            v_copy.wait()
            l_sc[...] = bcast(alpha * l_sc[:, 0:1] + jnp.sum(p, axis=1, keepdims=True))
            for h in range(N_HEADS):
                pv = jnp.sum(vbuf[h] * p[h:h + 1, :], axis=1, keepdims=True)
                acc_sc[h] = alpha[h:h + 1, :] * acc_sc[h] + pv
            m_sc[...] = bcast(m_new)

        suf_sc[...] = bcast(suf + page_sum)
        return carry

    lax.fori_loop(0, n_pages, body, 0)
    for h in range(N_HEADS):
        o_ref[0, h] = acc_sc[h] / l_sc[h:h + 1, 0:1]


def _attn_sample(q, cache_k, cache_v, cache_logf, page_table, k_new, v_new, lf_new):
    nseq, n_pages = page_table.shape
    page = cache_k.shape[1]
    tri = np.triu(np.ones((page, page), np.float32))
    heads = lambda a: a.reshape(nseq, N_HEADS, HEAD_DIM)
    heads_col = lambda a: a.reshape(nseq, N_HEADS, HEAD_DIM, 1)
    per_seq = lambda *s: pl.BlockSpec((1,) + s, lambda b, pt: (b,) + (0,) * len(s))
    hbm = pl.BlockSpec(memory_space=pl.ANY)
    grid_spec = pltpu.PrefetchScalarGridSpec(
        num_scalar_prefetch=1,
        grid=(nseq,),
        in_specs=[per_seq(N_HEADS, HEAD_DIM), per_seq(N_HEADS, HEAD_DIM, 1), per_seq(N_HEADS, HEAD_DIM),
                  per_seq(N_HEADS, HEAD_DIM, 1), per_seq(N_HEADS, 1),
                  pl.BlockSpec((page, page), lambda b, pt: (0, 0)), hbm, hbm, hbm],
        out_specs=per_seq(N_HEADS, HEAD_DIM, 1),
        scratch_shapes=[pltpu.VMEM((2, N_HEADS, HEAD_DIM, page), F32), pltpu.VMEM((2, N_HEADS, page), F32),
                        pltpu.VMEM((N_HEADS, HEAD_DIM, page), F32),
                        pltpu.SemaphoreType.DMA((2,)), pltpu.SemaphoreType.DMA((2,)), pltpu.SemaphoreType.DMA(()),
                        pltpu.VMEM((N_HEADS, LANES), F32), pltpu.VMEM((N_HEADS, LANES), F32),
                        pltpu.VMEM((N_HEADS, HEAD_DIM, 1), F32), pltpu.VMEM((N_HEADS, LANES), F32)])
    out = pl.pallas_call(
        _attn_sample_kernel, grid_spec=grid_spec,
        out_shape=jax.ShapeDtypeStruct((nseq, N_HEADS, HEAD_DIM, 1), F32),
        compiler_params=_cparams(("arbitrary",), 32),
        name="attn_sample",
    )(page_table, heads(q), heads_col(q), heads(k_new), heads_col(v_new), lf_new.reshape(nseq, N_HEADS, 1),
      jnp.asarray(tri), cache_k.transpose(0, 2, 3, 1), cache_v.transpose(0, 2, 3, 1), cache_logf.transpose(0, 2, 1))
    return out.reshape(nseq, D_MODEL)


def _wo_kernel(x_ref, a_ref, w_ref, o_ref, *, hi):
    o_ref[...] = x_ref[...] + _dot(a_ref[...], w_ref[...], hi)


def _wo(x, attn, w, *, tile, hi):
    n = x.shape[0]
    row = pl.BlockSpec((tile, D_MODEL), lambda i: (i, 0))
    return pl.pallas_call(
        functools.partial(_wo_kernel, hi=hi),
        grid=(n // tile,),
        in_specs=[row, row, pl.BlockSpec((D_MODEL, D_MODEL), lambda i: (0, 0))],
        out_specs=row,
        out_shape=jax.ShapeDtypeStruct((n, D_MODEL), F32),
        compiler_params=_cparams(("arbitrary",), 32),
        name="wo_hi" if hi else "wo",
    )(x, attn, w if hi else w.astype(BF16))


MOE_TILE = 1024
PROJ_TILE = 256
WO_TILE = 512


def kernel(x_prompt, x_sample, state_ssm_re, state_ssm_im, cache_k, cache_v, cache_logf, page_table, norm_mix, norm_ffn, ssm_lambda_re, ssm_lambda_im, ssm_log_dt, ssm_b_re, ssm_b_im, ssm_c_re, ssm_c_im, ssm_d, ssm_w_glu, ssm_b_glu, kv_norm, w_kvf, b_forget, k_norm, w_q, q_norm, w_o, w_router_group, w_router_expert, w_gate, w_up, w_down):
    t = x_prompt.shape[1]
    nseq = x_sample.shape[0]
    hd = N_HEADS * HEAD_DIM
    xp = x_prompt.reshape(t, D_MODEL)
    xs = x_sample.reshape(nseq, D_MODEL)

    apow_re, apow_im, bb_re, bb_im = _ssm_prep(ssm_lambda_re[0], ssm_lambda_im[0], ssm_log_dt[0],
                                               ssm_b_re[0], ssm_b_im[0])
    bmat, cmat, consts, a_rows = _ssm_layouts(apow_re, apow_im, bb_re, bb_im, ssm_c_re[0], ssm_c_im[0])
    wk, wv, wf = w_kvf[:, :hd], w_kvf[:, hd:2 * hd], w_kvf[:, 2 * hd:]

    def moe(x, layer, tile, hi):
        return _moe(x, norm_ffn[layer], w_router_group[layer], w_router_expert[layer],
                    w_gate, w_up, w_down, layer=layer, tile=tile, hi=hi)

    x1, s_end = _s5_prompt(xp, norm_mix[0], bmat, cmat, consts, ssm_d[0], ssm_w_glu[0], ssm_b_glu[0])
    x2 = moe(x1, 0, MOE_TILE, False)
    k_p, v_p, lf_p, q_p, kb_p, vb_p, cum_t = _proj(x2, kv_norm, norm_mix[1], wk, wv, wf, w_q[0], k_norm, q_norm[0],
                                                  b_forget, tile=PROJ_TILE, hi=False, with_cum=True)
    jstart = _first_key_block(cum_t, q_norm[0], k_norm)
    attn_p = _attn_prompt(q_p, kb_p, vb_p, cum_t, jstart)
    x3 = _wo(x2, attn_p, w_o[0], tile=WO_TILE, hi=False)
    y_p = moe(x3, 1, MOE_TILE, False)
    sre_p, sim_p = _state_from_layout(s_end)

    s0 = _state_to_layout(state_ssm_re[0], state_ssm_im[0])
    x1s, s_new = _s5_sample(xs, norm_mix[0], bmat, cmat, a_rows, s0, ssm_d[0], ssm_w_glu[0], ssm_b_glu[0])
    x2s = moe(x1s, 0, nseq, True)
    k_s, v_s, lf_s, q_s = _proj(x2s, kv_norm, norm_mix[1], wk, wv, wf, w_q[0], k_norm, q_norm[0], b_forget,
                                tile=nseq, hi=True, with_cum=False)
    attn_s = _attn_sample(q_s, cache_k, cache_v, cache_logf, page_table, k_s, v_s, lf_s)
    x3s = _wo(x2s, attn_s, w_o[0], tile=nseq, hi=True)
    y_s = moe(x3s, 1, nseq, True)
    sre_s, sim_s = _state_from_layout(s_new)

    return (y_p.reshape(1, t, D_MODEL), y_s.reshape(nseq, 1, D_MODEL),
            sre_p[None], sim_p[None],
            k_p.reshape(1, t, N_HEADS, HEAD_DIM), v_p.reshape(1, t, N_HEADS, HEAD_DIM), lf_p.reshape(1, t, N_HEADS),
            sre_s[None], sim_s[None],
            k_s.reshape(nseq, 1, N_HEADS, HEAD_DIM), v_s.reshape(nseq, 1, N_HEADS, HEAD_DIM),
            lf_s.reshape(nseq, 1, N_HEADS))
```
